```python
import math
import jax, jax.numpy as jnp
from jax import lax
import numpy as np

D_MODEL = 1024
BATCH = 32
SEQ = 2048
DEPTH = 1

N_META = 16
HG_HEADS = 4
HG_HEAD_DIM = 128
D_HG = HG_HEADS * HG_HEAD_DIM
HG_CHUNK = 64
D_LRU = D_MODEL
LRU_BLOCKS = 8
LRU_BLOCK = D_LRU // LRU_BLOCKS
CONV_WIDTH = 4
LRU_C = 8.0
N_GROUPS = 4
EXPERTS_PER_GROUP = 8
N_EXPERTS = N_GROUPS * EXPERTS_PER_GROUP
TOP_K = 2
D_EXPERT = 512
MOE_BLOCK = 512
EPS = 1e-6
SPLIT_SIZES = (D_HG, D_HG, D_HG, D_HG, D_LRU, D_LRU, D_MODEL, D_MODEL)
D_IN = sum(SPLIT_SIZES)

kernel_name = "hybrid_hgrn2_rglru_hmoe_block"


def rmsnorm(x, g):
    xf = x.astype(jnp.float32)
    y = xf * lax.rsqrt(jnp.mean(xf * xf, axis=-1, keepdims=True) + EPS)
    return (y * g.astype(jnp.float32)).astype(x.dtype)


def hgrn2_chunked(q, k, v, log_f):
    B, T, H, DK = q.shape
    DV = v.shape[-1]
    C = HG_CHUNK
    NC = T // C

    def to_chunks(a):
        return a.reshape(B, NC, C, H, a.shape[-1]).transpose(1, 0, 3, 2, 4)

    qc, kc, vc, gc = to_chunks(q), to_chunks(k), to_chunks(v), to_chunks(log_f)
    causal = jnp.tril(jnp.ones((C, C), dtype=bool))[:, :, None]

    def step(S, inp):
        qb, kb, vb, gb = inp
        G = jnp.cumsum(gb, axis=2)
        diff = G[:, :, :, None, :] - G[:, :, None, :, :]
        decay = jnp.exp(jnp.where(causal, diff, -jnp.inf))
        scores = jnp.einsum('bhtd,bhsd,bhtsd->bhts', qb, kb, decay)
        o = (jnp.einsum('bhts,bhsv->bhtv', scores, vb)
             + jnp.einsum('bhtd,bhdv->bhtv', qb * jnp.exp(G), S))
        G_end = G[:, :, -1:, :]
        S_new = (jnp.exp(G_end[:, :, 0, :])[..., None] * S
                 + jnp.einsum('bhsd,bhsv->bhdv', kb * jnp.exp(G_end - G), vb))
        return S_new, o

    S0 = jnp.zeros((B, H, DK, DV), jnp.float32)
    _, o = lax.scan(step, S0, (qc, kc, vc, gc))
    return o.transpose(1, 0, 3, 2, 4).reshape(B, T, H, DV)


def hgrn2_branch(q_pre, f_pre, i_pre, og_pre, lb, norm_g):
    B, T, _ = q_pre.shape
    f32 = jnp.float32
    lb = lb.astype(f32)
    fp = f_pre.astype(f32)
    f = lb + (1.0 - lb) * jax.nn.sigmoid(fp)
    log_f = jnp.log(f)
    k = (1.0 - lb) * jax.nn.sigmoid(-fp)
    pad = (-N_META) % HG_CHUNK

    def prep(a):
        a = jnp.pad(a.astype(f32), ((0, 0), (pad, 0), (0, 0)))
        return a.reshape(B, T + pad, HG_HEADS, HG_HEAD_DIM)

    o = hgrn2_chunked(prep(q_pre), prep(k), prep(i_pre), prep(log_f))[:, pad:]
    o = o * lax.rsqrt(jnp.mean(o * o, axis=-1, keepdims=True) + EPS)
    o = o * norm_g.astype(f32).reshape(HG_HEADS, HG_HEAD_DIM)
    o = o.reshape(B, T, D_HG) * jax.nn.sigmoid(og_pre.astype(f32))
    return o.astype(q_pre.dtype)


def rglru_branch(xb, yb, conv_w, conv_b, w_r, b_r, w_i, b_i, lam):
    B, T, _ = xb.shape
    f32 = jnp.float32
    xc = lax.conv_general_dilated(
        xb, conv_w[:, None, :].astype(xb.dtype), window_strides=(1,),
        padding=[(CONV_WIDTH - 1, 0)], dimension_numbers=('NWC', 'WIO', 'NWC'),
        feature_group_count=D_LRU) + conv_b.astype(xb.dtype)
    xh = xc.reshape(B, T, LRU_BLOCKS, LRU_BLOCK)
    r = jax.nn.sigmoid(jnp.einsum('btni,nij->btnj', xh, w_r).reshape(B, T, D_LRU).astype(f32)
                       + b_r.astype(f32))
    i = jax.nn.sigmoid(jnp.einsum('btni,nij->btnj', xh, w_i).reshape(B, T, D_LRU).astype(f32)
                       + b_i.astype(f32))
    log_a = -LRU_C * r * jax.nn.softplus(-lam.astype(f32))
    a = jnp.exp(log_a)
    u = jnp.sqrt(-jnp.expm1(2.0 * log_a)) * (i * xc.astype(f32))

    def combine(left, right):
        a_l, b_l = left
        a_r, b_r2 = right
        return a_l * a_r, a_r * b_l + b_r2

    _, h = lax.associative_scan(combine, (a, u), axis=1)
    return (h * jax.nn.gelu(yb.astype(f32))).astype(xb.dtype)


def hier_moe(h, w_group, b_group, w_router, b_router, w_gate, w_up, w_down):
    B, T, D = h.shape
    N = B * T
    f32 = jnp.float32
    hf = h.reshape(N, D)
    group_logits = (hf @ w_group).astype(f32) + b_group.astype(f32)
    p_group = jax.nn.softmax(group_logits, axis=-1)
    g_sel = jnp.argmax(group_logits, axis=-1).astype(jnp.int32)
    p_sel = jnp.take_along_axis(p_group, g_sel[:, None], axis=-1)
    exp_logits = jnp.einsum('nd,gde->nge', hf, w_router).astype(f32) + b_router.astype(f32)
    sel_logits = jnp.take_along_axis(exp_logits, g_sel[:, None, None], axis=1)[:, 0]
    top_vals, top_idx = lax.top_k(sel_logits, TOP_K)
    weights = p_sel * jax.nn.softmax(top_vals, axis=-1)
    expert_ids = g_sel[:, None] * EXPERTS_PER_GROUP + top_idx.astype(jnp.int32)

    A = N * TOP_K
    e_flat = expert_ids.reshape(A)
    w_flat = weights.reshape(A)
    tok_flat = jnp.arange(A, dtype=jnp.int32) // TOP_K
    order = jnp.argsort(e_flat)
    e_sorted = e_flat[order]
    counts = jnp.zeros((N_EXPERTS,), jnp.int32).at[e_flat].add(1)
    starts = jnp.cumsum(counts) - counts
    pcounts = (counts + MOE_BLOCK - 1) // MOE_BLOCK * MOE_BLOCK
    pends = jnp.cumsum(pcounts)
    pstarts = pends - pcounts
    dest = pstarts[e_sorted] + (jnp.arange(A, dtype=jnp.int32) - starts[e_sorted])
    n_blocks = -(-A // MOE_BLOCK) + N_EXPERTS
    n_slots = n_blocks * MOE_BLOCK
    slot_tok = jnp.zeros((n_slots,), jnp.int32).at[dest].set(tok_flat[order])
    slot_w = jnp.zeros((n_slots,), f32).at[dest].set(w_flat[order])
    block_start = jnp.arange(n_blocks, dtype=jnp.int32) * MOE_BLOCK
    block_expert = jnp.minimum(jnp.searchsorted(pends, block_start, side='right'),
                               N_EXPERTS - 1).astype(jnp.int32)

    def run_block(args):
        toks, wts, e = args
        xb = hf[toks]
        y = (jax.nn.silu(xb @ w_gate[e]) * (xb @ w_up[e])) @ w_down[e]
        return y * wts[:, None].astype(y.dtype)

    ys = lax.map(run_block, (slot_tok.reshape(n_blocks, MOE_BLOCK),
                             slot_w.reshape(n_blocks, MOE_BLOCK), block_expert))
    out = jnp.zeros((N, D), h.dtype).at[slot_tok].add(ys.reshape(n_slots, D).astype(h.dtype))
    return out.reshape(B, T, D)


def mixer(hn, w_in, lb, hg_norm_g, conv_w, conv_b, w_r, b_r, w_i, b_i, lam, w_up_a, w_up_b, w_out):
    proj = hn @ w_in
    idx = [int(c) for c in np.cumsum(SPLIT_SIZES)[:-1]]
    q_pre, f_pre, i_pre, og_pre, lx, ly, ga, gb = jnp.split(proj, idx, axis=-1)
    o_a = hgrn2_branch(q_pre, f_pre, i_pre, og_pre, lb, hg_norm_g)
    o_b = rglru_branch(lx, ly, conv_w, conv_b, w_r, b_r, w_i, b_i, lam)
    merged = jax.nn.sigmoid(ga) * (o_a @ w_up_a) + jax.nn.sigmoid(gb) * (o_b @ w_up_b)
    return merged @ w_out


def setup_inputs(seed: int = 0) -> dict:
    key = jax.random.key(seed)
    ks = jax.random.split(key, 26)
    f32 = jnp.float32
    nrm = lambda k, shape, scale: jax.random.normal(k, shape, f32) * scale
    u = jax.random.uniform(ks[13], (DEPTH, D_LRU), f32, minval=0.9, maxval=0.999)
    a0 = u ** (1.0 / LRU_C)
    return {
        "x": nrm(ks[0], (BATCH, SEQ, D_MODEL), 1.0),
        "meta_tokens": nrm(ks[1], (N_META, D_MODEL), 1.0),
        "norm1_g": 1.0 + nrm(ks[2], (DEPTH, D_MODEL), 0.02),
        "w_in": nrm(ks[3], (DEPTH, D_MODEL, D_IN), D_MODEL ** -0.5),
        "hg_lower_bounds": nrm(ks[4], (DEPTH + 1, D_HG), 0.1),
        "hg_norm_g": 1.0 + nrm(ks[5], (DEPTH, D_HG), 0.02),
        "conv_w": nrm(ks[6], (DEPTH, CONV_WIDTH, D_LRU), CONV_WIDTH ** -0.5),
        "conv_b": nrm(ks[7], (DEPTH, D_LRU), 0.01),
        "lru_w_r": nrm(ks[8], (DEPTH, LRU_BLOCKS, LRU_BLOCK, LRU_BLOCK), LRU_BLOCK ** -0.5),
        "lru_b_r": nrm(ks[9], (DEPTH, D_LRU), 0.01),
        "lru_w_i": nrm(ks[10], (DEPTH, LRU_BLOCKS, LRU_BLOCK, LRU_BLOCK), LRU_BLOCK ** -0.5),
        "lru_b_i": nrm(ks[11], (DEPTH, D_LRU), 0.01),
        "lru_lambda": jnp.log(a0) - jnp.log1p(-a0),
        "w_up_a": nrm(ks[14], (DEPTH, D_HG, D_MODEL), D_HG ** -0.5),
        "w_up_b": nrm(ks[15], (DEPTH, D_LRU, D_MODEL), D_LRU ** -0.5),
        "w_out": nrm(ks[16], (DEPTH, D_MODEL, D_MODEL), D_MODEL ** -0.5),
        "norm2_g": 1.0 + nrm(ks[17], (DEPTH, D_MODEL), 0.02),
        "w_group": nrm(ks[18], (DEPTH, D_MODEL, N_GROUPS), D_MODEL ** -0.5),
        "b_group": nrm(ks[19], (DEPTH, N_GROUPS), 0.01),
        "w_router": nrm(ks[20], (DEPTH, N_GROUPS, D_MODEL, EXPERTS_PER_GROUP), D_MODEL ** -0.5),
        "b_router": nrm(ks[21], (DEPTH, N_GROUPS, EXPERTS_PER_GROUP), 0.01),
        "w_gate": nrm(ks[22], (DEPTH, N_EXPERTS, D_MODEL, D_EXPERT), D_MODEL ** -0.5),
        "w_up": nrm(ks[23], (DEPTH, N_EXPERTS, D_MODEL, D_EXPERT), D_MODEL ** -0.5),
        "w_down": nrm(ks[24], (DEPTH, N_EXPERTS, D_EXPERT, D_MODEL), D_EXPERT ** -0.5),
        "final_g": 1.0 + nrm(ks[25], (D_MODEL,), 0.02),
    }


def reference(x, meta_tokens, norm1_g, w_in, hg_lower_bounds, hg_norm_g, conv_w, conv_b,
              lru_w_r, lru_b_r, lru_w_i, lru_b_i, lru_lambda, w_up_a, w_up_b, w_out,
              norm2_g, w_group, b_group, w_router, b_router, w_gate, w_up, w_down, final_g):
    B = x.shape[0]
    meta = jnp.broadcast_to(meta_tokens[None].astype(x.dtype), (B, N_META, D_MODEL))
    h = jnp.concatenate([meta, x], axis=1)
    lbs = jnp.cumsum(jax.nn.softmax(hg_lower_bounds.astype(jnp.float32), axis=0), axis=0)
    for l in range(DEPTH):
        mix = mixer(rmsnorm(h, norm1_g[l]), w_in[l], lbs[l], hg_norm_g[l], conv_w[l], conv_b[l],
                    lru_w_r[l], lru_b_r[l], lru_w_i[l], lru_b_i[l], lru_lambda[l],
                    w_up_a[l], w_up_b[l], w_out[l])
        h = h + mix.astype(h.dtype)
        ffn = hier_moe(rmsnorm(h, norm2_g[l]), w_group[l], b_group[l], w_router[l], b_router[l],
                       w_gate[l], w_up[l], w_down[l])
        h = h + ffn.astype(h.dtype)
    return rmsnorm(h, final_g)[:, N_META:]
```

```python
import functools

import jax
import jax.numpy as jnp
from jax import lax
from jax.experimental import pallas as pl
from jax.experimental.pallas import tpu as pltpu

F32 = jnp.float32
BF16 = jnp.bfloat16
EPS = 1e-6
LRU_C = 8.0
CONV_WIDTH = 4
HG_CHUNK = 64
HG_SUB = 16
N_SUB = HG_CHUNK // HG_SUB
LANES = 128
SUBLANES = 8
MOE_ROWS = 256
VMEM_LIMIT_BYTES = 56 * 1024 * 1024

_NT = (((1,), (1,)), ((), ()))
_TN = (((0,), (0,)), ((), ()))


def _cparams(n_axes):
    return pltpu.CompilerParams(dimension_semantics=("arbitrary",) * n_axes,
                                vmem_limit_bytes=VMEM_LIMIT_BYTES)


def _dot(a, b):
    return jnp.dot(a, b, preferred_element_type=F32)


def _sigmoid(x):
    return 1.0 / (1.0 + jnp.exp(-x))


def _rms(x, g):
    return x * lax.rsqrt(jnp.mean(x * x, axis=-1, keepdims=True) + EPS) * g


def _proj_kernel(x_ref, g_ref, w_ref, o_ref):
    xn = _rms(x_ref[...], g_ref[...]).astype(BF16)
    o_ref[...] = _dot(xn, w_ref[...])


def _proj(xf, g, w_bf16, tm):
    n, d = xf.shape
    d_in = w_bf16.shape[1]
    return pl.pallas_call(
        _proj_kernel,
        grid=(n // tm,),
        in_specs=[pl.BlockSpec((tm, d), lambda i: (i, 0)),
                  pl.BlockSpec((1, d), lambda i: (0, 0)),
                  pl.BlockSpec((d, d_in), lambda i: (0, 0), pipeline_mode=pl.Buffered(1))],
        out_specs=pl.BlockSpec((tm, d_in), lambda i: (i, 0)),
        out_shape=jax.ShapeDtypeStruct((n, d_in), F32),
        compiler_params=_cparams(1),
        name="proj",
    )(xf, g, w_bf16)


def _split3(x):
    hi = x.astype(BF16)
    r1 = x - hi.astype(F32)
    mid = r1.astype(BF16)
    lo = (r1 - mid.astype(F32)).astype(BF16)
    return hi, mid, lo


def _hgrn_kernel(q_ref, f_ref, i_ref, og_ref, lb_ref, ng_ref, s0_ref, o_ref, sfin_ref,
                 st_ref, g_ref, *, n_chunks, n_heads, dk):
    c = pl.program_id(1)

    @pl.when(c == 0)
    def _():
        st_ref[...] = s0_ref[...]

    lb = lb_ref[...]
    ng = ng_ref[...]
    row = lax.broadcasted_iota(jnp.int32, (HG_CHUNK, 1), 0)
    rr = lax.broadcasted_iota(jnp.int32, (HG_CHUNK, HG_CHUNK), 0)
    cc = lax.broadcasted_iota(jnp.int32, (HG_CHUNK, HG_CHUNK), 1)
    causal = cc <= rr
    tri = jnp.where(causal, 1.0, 0.0).astype(BF16)
    blk = row // HG_SUB

    def chunk(ch, carry):
        sl = pl.ds(pl.multiple_of(ch * HG_CHUNK, HG_CHUNK), HG_CHUNK)
        fp = f_ref[sl, :]
        q = q_ref[sl, :]
        v = i_ref[sl, :]
        sg = _sigmoid(fp)
        logf = jnp.log(lb + (1.0 - lb) * sg)
        k = (1.0 - lb) * (1.0 - sg)
        hi, mid, lo = _split3(logf)
        g_ref[...] = _dot(tri, hi) + _dot(tri, mid) + _dot(tri, lo)
        gc = g_ref[...]
        gend = g_ref[pl.ds(HG_CHUNK - 1, 1), :]
        bases = [jnp.zeros_like(gend)] + [g_ref[pl.ds(HG_SUB * i - 1, 1), :] for i in range(1, N_SUB)]
        bvec = bases[N_SUB - 1]
        for i in range(N_SUB - 2, -1, -1):
            bvec = jnp.where(blk == i, bases[i], bvec)
        qhat = q * jnp.exp(gc - bvec)
        kts = [jnp.where(row < HG_SUB * (i + 1), k * jnp.exp(bases[i] - gc), 0.0).astype(BF16)
               for i in range(N_SUB)]
        qi = (q * jnp.exp(gc)).astype(BF16)
        ks = (k * jnp.exp(gend - gc)).astype(BF16)
        dec = jnp.exp(gend)
        vb = v.astype(BF16)
        outs = []
        for h in range(n_heads):
            ls = slice(h * dk, (h + 1) * dk)
            qh = qhat[:, ls]
            qt = jnp.concatenate([jnp.where(blk == i, qh, 0.0) for i in range(N_SUB)], axis=1).astype(BF16)
            kt = jnp.concatenate([kts[i][:, ls] for i in range(N_SUB)], axis=1)
            sc = lax.dot_general(qt, kt, _NT, preferred_element_type=F32)
            p = jnp.where(causal, sc, 0.0).astype(BF16)
            st = st_ref[h]
            o_h = _dot(p, vb[:, ls]) + lax.dot_general(qi[:, ls], st.astype(BF16), _NT,
                                                       preferred_element_type=F32)
            st_ref[h] = st * dec[:, ls] + lax.dot_general(vb[:, ls], ks[:, ls], _TN,
                                                          preferred_element_type=F32)
            outs.append(o_h * lax.rsqrt(jnp.mean(o_h * o_h, axis=-1, keepdims=True) + EPS))
        o = jnp.concatenate(outs, axis=1)
        o_ref[sl, :] = o * ng * _sigmoid(og_ref[sl, :])
        return carry

    lax.fori_loop(0, n_chunks, chunk, 0)

    @pl.when(c == pl.num_programs(1) - 1)
    def _():
        sfin_ref[...] = st_ref[...]


def _hgrn(proj, lb, ng, s0, batch, t, tt, n_heads, dk):
    d_hg = n_heads * dk
    nc = t // tt
    col = lambda j: pl.BlockSpec((tt, d_hg), lambda b, c, j=j: (b * nc + c, j))
    kern = functools.partial(_hgrn_kernel, n_chunks=tt // HG_CHUNK, n_heads=n_heads, dk=dk)
    return pl.pallas_call(
        kern,
        grid=(batch, nc),
        in_specs=[col(0), col(1), col(2), col(3),
                  pl.BlockSpec((1, d_hg), lambda b, c: (0, 0)),
                  pl.BlockSpec((1, d_hg), lambda b, c: (0, 0)),
                  pl.BlockSpec((n_heads, dk, dk), lambda b, c: (0, 0, 0))],
        out_specs=[pl.BlockSpec((tt, d_hg), lambda b, c: (b * nc + c, 0)),
                   pl.BlockSpec((None, n_heads, dk, dk), lambda b, c: (b, 0, 0, 0))],
        out_shape=[jax.ShapeDtypeStruct((batch * t, d_hg), F32),
                   jax.ShapeDtypeStruct((batch, n_heads, dk, dk), F32)],
        scratch_shapes=[pltpu.VMEM((n_heads, dk, dk), F32),
                        pltpu.VMEM((HG_CHUNK, d_hg), F32)],
        compiler_params=_cparams(2),
        name="hgrn",
    )(proj, proj, proj, proj, lb, ng, s0)


def _softplus(x):
    return jnp.maximum(x, 0.0) + jnp.log(1.0 + jnp.exp(-jnp.abs(x)))


def _gelu_tanh(x):
    return 0.5 * x * (1.0 + jnp.tanh(0.7978845608028654 * (x + 0.044715 * (x * x * x))))


def _lru_kernel(lx_ref, ly_ref, cw_ref, cb_ref, wri_ref, br_ref, bi_ref, lam_ref, h0_ref, c0_ref,
                o_ref, hfin_ref, cfin_ref, ext_ref, h_ref, *, tt, n_blocks, bw, valid_from):
    c = pl.program_id(1)
    d = n_blocks * bw

    @pl.when(c == 0)
    def _():
        ext_ref[pl.ds(0, SUBLANES), :] = c0_ref[...]
        h_ref[...] = h0_ref[...]

    lx = lx_ref[...]
    ext_ref[pl.ds(SUBLANES, tt), :] = lx
    xc = cb_ref[...] + cw_ref[pl.ds(CONV_WIDTH - 1, 1), :] * lx
    for j in range(CONV_WIDTH - 1):
        xc = xc + cw_ref[pl.ds(j, 1), :] * ext_ref[pl.ds(SUBLANES - (CONV_WIDTH - 1) + j, tt), :]
    ext_ref[pl.ds(0, SUBLANES), :] = lx[tt - SUBLANES:, :]

    xcb = xc.astype(BF16)
    rs, is_ = [], []
    for n in range(n_blocks):
        ri = _dot(xcb[:, n * bw:(n + 1) * bw], wri_ref[n])
        rs.append(ri[:, :bw])
        is_.append(ri[:, bw:])
    r = _sigmoid(jnp.concatenate(rs, axis=1) + br_ref[...])
    ig = _sigmoid(jnp.concatenate(is_, axis=1) + bi_ref[...])
    log_a = (-LRU_C) * r * _softplus(-lam_ref[...])
    a = jnp.exp(log_a)
    th = jnp.tanh(log_a)
    u = jnp.sqrt(-2.0 * th / (1.0 - th)) * (ig * xc)
    row = lax.broadcasted_iota(jnp.int32, (tt, 1), 0)
    if valid_from:
        u = jnp.where(row >= valid_from, u, 0.0)
    s = 1
    while s < tt:
        keep = row >= s
        a_s = jnp.where(keep, pltpu.roll(a, s, 0), 1.0)
        u_s = jnp.where(keep, pltpu.roll(u, s, 0), 0.0)
        u = a * u_s + u
        a = a * a_s
        s *= 2
    h = u + a * h_ref[...]
    h_ref[...] = h[tt - 1:, :]
    o_ref[...] = h * _gelu_tanh(ly_ref[...])

    @pl.when(c == pl.num_programs(1) - 1)
    def _():
        hfin_ref[...] = h_ref[...]
        cfin_ref[...] = ext_ref[pl.ds(0, SUBLANES), :]


def _lru(proj, cw, cb, wri, br, bi, lam, h0, c0, batch, t, tt, col0, valid_from):
    n_blocks, bw, _ = wri.shape
    d = n_blocks * bw
    nc = t // tt
    vec = pl.BlockSpec((1, d), lambda b, c: (0, 0))
    kern = functools.partial(_lru_kernel, tt=tt, n_blocks=n_blocks, bw=bw, valid_from=valid_from)
    return pl.pallas_call(
        kern,
        grid=(batch, nc),
        in_specs=[pl.BlockSpec((tt, d), lambda b, c: (b * nc + c, col0)),
                  pl.BlockSpec((tt, d), lambda b, c: (b * nc + c, col0 + 1)),
                  pl.BlockSpec((CONV_WIDTH, d), lambda b, c: (0, 0)),
                  vec,
                  pl.BlockSpec((n_blocks, bw, 2 * bw), lambda b, c: (0, 0, 0)),
                  vec, vec, vec, vec,
                  pl.BlockSpec((SUBLANES, d), lambda b, c: (0, 0))],
        out_specs=[pl.BlockSpec((tt, d), lambda b, c: (b * nc + c, 0)),
                   pl.BlockSpec((None, 1, d), lambda b, c: (b, 0, 0)),
                   pl.BlockSpec((None, SUBLANES, d), lambda b, c: (b, 0, 0))],
        out_shape=[jax.ShapeDtypeStruct((batch * t, d), F32),
                   jax.ShapeDtypeStruct((batch, 1, d), F32),
                   jax.ShapeDtypeStruct((batch, SUBLANES, d), F32)],
        scratch_shapes=[pltpu.VMEM((tt + SUBLANES, d), F32),
                        pltpu.VMEM((1, d), F32)],
        compiler_params=_cparams(2),
        name="lru",
    )(proj, proj, cw, cb, wri, br, bi, lam, h0, c0)


def _merge_kernel(x_ref, oa_ref, ob_ref, ga_ref, gb_ref, wa_ref, wb_ref, wo_ref, g2_ref,
                  wrh_ref, wrl_ref, br_ref, h2_ref, hn_ref, route_ref, *, n_groups, per_group):
    up_a = _dot(oa_ref[...].astype(BF16), wa_ref[...])
    up_b = _dot(ob_ref[...].astype(BF16), wb_ref[...])
    merged = _sigmoid(ga_ref[...]) * up_a + _sigmoid(gb_ref[...]) * up_b
    h2 = x_ref[...] + _dot(merged.astype(BF16), wo_ref[...])
    h2_ref[...] = h2
    hn = _rms(h2, g2_ref[...])
    hn_ref[...] = hn
    hh = hn.astype(BF16)
    hl = (hn - hh.astype(F32)).astype(BF16)
    logits = _dot(hh, wrh_ref[...]) + _dot(hh, wrl_ref[...]) + _dot(hl, wrh_ref[...]) + br_ref[...]
    lane = lax.broadcasted_iota(jnp.int32, logits.shape, 1).astype(F32)
    neg = -jnp.inf
    big = float(LANES)

    def first_argmax(vals):
        m = jnp.max(vals, axis=1, keepdims=True)
        return m, jnp.min(jnp.where(vals == m, lane, big), axis=1, keepdims=True)

    is_g = lane < n_groups
    gmax, gidx = first_argmax(jnp.where(is_g, logits, neg))
    psel = 1.0 / jnp.sum(jnp.where(is_g, jnp.exp(logits - gmax), 0.0), axis=1, keepdims=True)
    lo = n_groups + per_group * gidx
    el = jnp.where((lane >= lo) & (lane < lo + per_group), logits, neg)
    v1, i1 = first_argmax(el)
    v2, i2 = first_argmax(jnp.where(lane == i1, neg, el))
    t = jnp.exp(v2 - v1)
    w1 = psel / (1.0 + t)
    w2 = psel * t / (1.0 + t)
    route = jnp.where(lane == 0, i1 - n_groups,
                      jnp.where(lane == 1, i2 - n_groups,
                                jnp.where(lane == 2, w1, jnp.where(lane == 3, w2, 0.0))))
    route_ref[...] = route


def _merge(xf, o_a, o_b, proj, wa, wb, wo, g2, wrh, wrl, br, tm, gate_col0, n_groups, per_group):
    n, d = xf.shape
    d_hg = o_a.shape[1]
    d_lru = o_b.shape[1]
    full = lambda shape: pl.BlockSpec(shape, lambda i: (0,) * len(shape))
    rows = lambda w: pl.BlockSpec((tm, w), lambda i: (i, 0))
    kern = functools.partial(_merge_kernel, n_groups=n_groups, per_group=per_group)
    return pl.pallas_call(
        kern,
        grid=(n // tm,),
        in_specs=[rows(d), rows(d_hg), rows(d_lru),
                  pl.BlockSpec((tm, d), lambda i: (i, gate_col0)),
                  pl.BlockSpec((tm, d), lambda i: (i, gate_col0 + 1)),
                  full((d_hg, d)), full((d_lru, d)), full((d, d)), full((1, d)),
                  full((d, LANES)), full((d, LANES)), full((1, LANES))],
        out_specs=[rows(d), rows(d), rows(LANES)],
        out_shape=[jax.ShapeDtypeStruct((n, d), F32),
                   jax.ShapeDtypeStruct((n, d), F32),
                   jax.ShapeDtypeStruct((n, LANES), F32)],
        compiler_params=_cparams(1),
        name="merge",
    )(xf, o_a, o_b, proj, proj, wa, wb, wo, g2, wrh, wrl, br)


def _row_copy(src_hbm, row, dst, r, sem):
    return pltpu.make_async_copy(src_hbm.at[pl.ds(row, 1), :], dst.at[pl.ds(r, 1), :], sem)


def _moe_kernel(bexp_ref, bval_ref, tok_ref, x_hbm, wg_ref, wu_ref, wd_ref, y_ref, xbuf, sem, *, bm):
    m = pl.program_id(0)

    @pl.when(bval_ref[m] != 0)
    def _():
        def issue(r, carry):
            _row_copy(x_hbm, tok_ref[0, r], xbuf, r, sem).start()
            return carry

        lax.fori_loop(0, bm, issue, 0)

        def wait(r, carry):
            _row_copy(x_hbm, 0, xbuf, r, sem).wait()
            return carry

        lax.fori_loop(0, bm, wait, 0)
        xb = xbuf[...].astype(BF16)
        g = _dot(xb, wg_ref[...])
        u = _dot(xb, wu_ref[...])
        hm = (g * _sigmoid(g) * u).astype(BF16)
        y_ref[...] = _dot(hm, wd_ref[...])

    @pl.when(bval_ref[m] == 0)
    def _():
        y_ref[...] = jnp.zeros_like(y_ref)


def _moe(block_expert, block_valid, slot_tok, hn, wg, wu, wd, bm):
    n_blocks = block_expert.shape[0]
    d = hn.shape[1]
    de = wg.shape[2]
    grid_spec = pltpu.PrefetchScalarGridSpec(
        num_scalar_prefetch=2,
        grid=(n_blocks,),
        in_specs=[pl.BlockSpec((None, 1, bm), lambda m, be, bv: (m, 0, 0), memory_space=pltpu.SMEM),
                  pl.BlockSpec(memory_space=pl.ANY),
                  pl.BlockSpec((None, d, de), lambda m, be, bv: (be[m], 0, 0)),
                  pl.BlockSpec((None, d, de), lambda m, be, bv: (be[m], 0, 0)),
                  pl.BlockSpec((None, de, d), lambda m, be, bv: (be[m], 0, 0))],
        out_specs=pl.BlockSpec((bm, d), lambda m, be, bv: (m, 0)),
        scratch_shapes=[pltpu.VMEM((bm, d), F32), pltpu.SemaphoreType.DMA(())],
    )
    return pl.pallas_call(
        functools.partial(_moe_kernel, bm=bm),
        grid_spec=grid_spec,
        out_shape=jax.ShapeDtypeStruct((n_blocks * bm, d), F32),
        compiler_params=_cparams(1),
        name="moe",
    )(block_expert, block_valid, slot_tok, hn, wg, wu, wd)


def _combine_kernel(p1_ref, p2_ref, y_hbm, h2_ref, route_ref, fg_ref, o_ref, buf1, buf2, sem, *, tm):
    def issue(r, carry):
        _row_copy(y_hbm, p1_ref[0, r], buf1, r, sem).start()
        _row_copy(y_hbm, p2_ref[0, r], buf2, r, sem).start()
        return carry

    lax.fori_loop(0, tm, issue, 0)

    def wait(r, carry):
        _row_copy(y_hbm, 0, buf1, r, sem).wait()
        _row_copy(y_hbm, 0, buf2, r, sem).wait()
        return carry

    lax.fori_loop(0, tm, wait, 0)
    route = route_ref[...]
    ffn = buf1[...] * route[:, 2:3] + buf2[...] * route[:, 3:4]
    o_ref[...] = _rms(h2_ref[...] + ffn, fg_ref[...])


def _combine(pos1, pos2, y, h2, route, fg, tm):
    n, d = h2.shape
    smem_row = pl.BlockSpec((None, 1, tm), lambda i: (i, 0, 0), memory_space=pltpu.SMEM)
    return pl.pallas_call(
        functools.partial(_combine_kernel, tm=tm),
        grid=(n // tm,),
        in_specs=[smem_row, smem_row,
                  pl.BlockSpec(memory_space=pl.ANY),
                  pl.BlockSpec((tm, d), lambda i: (i, 0)),
                  pl.BlockSpec((tm, LANES), lambda i: (i, 0)),
                  pl.BlockSpec((1, d), lambda i: (0, 0))],
        out_specs=pl.BlockSpec((tm, d), lambda i: (i, 0)),
        out_shape=jax.ShapeDtypeStruct((n, d), F32),
        scratch_shapes=[pltpu.VMEM((tm, d), F32), pltpu.VMEM((tm, d), F32), pltpu.SemaphoreType.DMA(())],
        compiler_params=_cparams(1),
        name="combine",
    )(pos1, pos2, y, h2, route, fg)


def _dispatch(e_flat, n_experts, bm):
    a = e_flat.shape[0]
    n_blocks = a // bm + n_experts
    onehot = (e_flat[:, None] == jnp.arange(n_experts, dtype=jnp.int32)[None, :]).astype(jnp.int32)
    csum = jnp.cumsum(onehot, axis=0)
    counts = csum[-1]
    rank = jnp.take_along_axis(csum, e_flat[:, None], axis=1)[:, 0] - 1
    starts = jnp.cumsum(counts) - counts
    pcounts = (counts + bm - 1) // bm * bm
    pends = jnp.cumsum(pcounts)
    pstarts = pends - pcounts
    pos = pstarts[e_flat] + rank
    order = jnp.argsort(e_flat, stable=True).astype(jnp.int32)
    block_start = jnp.arange(n_blocks, dtype=jnp.int32) * bm
    block_expert = jnp.minimum(jnp.searchsorted(pends, block_start, side="right"),
                               n_experts - 1).astype(jnp.int32)
    block_valid = (block_start < pends[-1]).astype(jnp.int32)
    slot = jnp.arange(n_blocks * bm, dtype=jnp.int32)
    slot_e = jnp.repeat(block_expert, bm)
    off = slot - pstarts[slot_e]
    valid = off < counts[slot_e]
    src = jnp.clip(starts[slot_e] + off, 0, a - 1)
    slot_tok = jnp.where(valid, order[src] // 2, 0).astype(jnp.int32)
    return pos.astype(jnp.int32), slot_tok, block_expert, block_valid


def _tile(n, pref):
    t = pref
    while n % t:
        t //= 2
    return t


def kernel(x, meta_tokens, norm1_g, w_in, hg_lower_bounds, hg_norm_g, conv_w, conv_b, lru_w_r, lru_b_r,
           lru_w_i, lru_b_i, lru_lambda, w_up_a, w_up_b, w_out, norm2_g, w_group, b_group, w_router,
           b_router, w_gate, w_up, w_down, final_g):
    batch, t, d = x.shape
    n = batch * t
    n_meta = meta_tokens.shape[0]
    d_hg = hg_norm_g.shape[1]
    n_blocks_lru, bw = lru_w_r.shape[1], lru_w_r.shape[2]
    d_lru = n_blocks_lru * bw
    dk = LANES
    n_heads = d_hg // dk
    n_groups, per_group = w_router.shape[1], w_router.shape[3]
    n_experts = n_groups * per_group
    assert d_lru == d and d % LANES == 0 and t % HG_CHUNK == 0 and n_meta <= HG_CHUNK
    assert w_in.shape[2] == 4 * d_hg + 2 * d_lru + 2 * d
    assert n_groups + n_experts <= LANES

    layer = 0
    lbs = jnp.cumsum(jax.nn.softmax(hg_lower_bounds.astype(F32), axis=0), axis=0)
    lb = lbs[layer][None, :]
    g1 = norm1_g[layer][None, :]
    w_in_b = w_in[layer].astype(BF16)
    ng = hg_norm_g[layer][None, :]
    cw = conv_w[layer]
    cb = conv_b[layer][None, :]
    wri = jnp.concatenate([lru_w_r[layer], lru_w_i[layer]], axis=-1).astype(BF16)
    br = lru_b_r[layer][None, :]
    bi = lru_b_i[layer][None, :]
    lam = lru_lambda[layer][None, :]
    lx_col = (4 * d_hg) // d_lru
    gate_col = (4 * d_hg + 2 * d_lru) // d

    pad = HG_CHUNK - n_meta
    mseq = jnp.concatenate([jnp.zeros((pad, d), F32), meta_tokens.astype(F32)], axis=0)
    proj_m = _proj(mseq, g1, w_in_b, HG_CHUNK)
    _, s_meta = _hgrn(proj_m, lb, ng, jnp.zeros((n_heads, dk, dk), F32), 1, HG_CHUNK, HG_CHUNK, n_heads, dk)
    _, h_meta, c_meta = _lru(proj_m, cw, cb, wri, br, bi, lam, jnp.zeros((1, d_lru), F32),
                             jnp.zeros((SUBLANES, d_lru), F32), 1, HG_CHUNK, HG_CHUNK, lx_col, pad)

    xf = x.reshape(n, d)
    tm = _tile(n, 256)
    tt = _tile(t, 256)
    proj = _proj(xf, g1, w_in_b, tm)
    o_a, _ = _hgrn(proj, lb, ng, s_meta[0], batch, t, tt, n_heads, dk)
    o_b, _, _ = _lru(proj, cw, cb, wri, br, bi, lam, h_meta[0], c_meta[0], batch, t, tt, lx_col, 0)

    wr = jnp.concatenate([w_group[layer], jnp.transpose(w_router[layer], (1, 0, 2)).reshape(d, n_experts)], axis=1)
    wr = jnp.pad(wr, ((0, 0), (0, LANES - wr.shape[1])))
    wrh = wr.astype(BF16)
    wrl = (wr - wrh.astype(F32)).astype(BF16)
    brt = jnp.pad(jnp.concatenate([b_group[layer], b_router[layer].reshape(n_experts)]),
                  (0, LANES - n_groups - n_experts))[None, :]
    h2, hn, route = _merge(xf, o_a, o_b, proj, w_up_a[layer].astype(BF16), w_up_b[layer].astype(BF16),
                           w_out[layer].astype(BF16), norm2_g[layer][None, :], wrh, wrl, brt,
                           tm, gate_col, n_groups, per_group)

    e_flat = route[:, :2].astype(jnp.int32).reshape(2 * n)
    pos, slot_tok, block_expert, block_valid = _dispatch(e_flat, n_experts, MOE_ROWS)
    n_blocks = block_expert.shape[0]
    y = _moe(block_expert, block_valid, slot_tok.reshape(n_blocks, 1, MOE_ROWS), hn,
             w_gate[layer].astype(BF16), w_up[layer].astype(BF16), w_down[layer].astype(BF16), MOE_ROWS)
    pos2d = pos.reshape(n, 2)
    out = _combine(pos2d[:, 0].reshape(n // tm, 1, tm), pos2d[:, 1].reshape(n // tm, 1, tm),
                   y, h2, route, final_g[None, :], tm)
    return out.reshape(batch, t, d)
```

```python
import functools

import jax
import jax.numpy as jnp
from jax import lax
from jax.experimental import pallas as pl
from jax.experimental.pallas import tpu as pltpu

F32 = jnp.float32
BF16 = jnp.bfloat16
EPS = 1e-6
LRU_C = 8.0
CONV_WIDTH = 4
HG_CHUNK = 64
HG_SUB = 16
N_SUB = HG_CHUNK // HG_SUB
LANES = 128
SUBLANES = 8
MOE_ROWS = 256
VMEM_LIMIT_BYTES = 56 * 1024 * 1024

_NT = (((1,), (1,)), ((), ()))
_TN = (((0,), (0,)), ((), ()))


def _cparams(n_axes):
    return pltpu.CompilerParams(dimension_semantics=("arbitrary",) * n_axes,
                                vmem_limit_bytes=VMEM_LIMIT_BYTES)


def _dot(a, b):
    return jnp.dot(a, b, preferred_element_type=F32)


def _sigmoid(x):
    return 1.0 / (1.0 + jnp.exp(-x))


def _rms(x, g):
    return x * lax.rsqrt(jnp.mean(x * x, axis=-1, keepdims=True) + EPS) * g


def _proj_kernel(x_ref, g_ref, w_ref, o_ref):
    xn = _rms(x_ref[...], g_ref[...]).astype(BF16)
    o_ref[...] = _dot(xn, w_ref[...])


def _proj(xf, g, w_bf16, tm):
    n, d = xf.shape
    d_in = w_bf16.shape[1]
    return pl.pallas_call(
        _proj_kernel,
        grid=(n // tm,),
        in_specs=[pl.BlockSpec((tm, d), lambda i: (i, 0)),
                  pl.BlockSpec((1, d), lambda i: (0, 0)),
                  pl.BlockSpec((d, d_in), lambda i: (0, 0), pipeline_mode=pl.Buffered(1))],
        out_specs=pl.BlockSpec((tm, d_in), lambda i: (i, 0)),
        out_shape=jax.ShapeDtypeStruct((n, d_in), F32),
        compiler_params=_cparams(1),
        name="proj",
    )(xf, g, w_bf16)


def _split3(x):
    hi = x.astype(BF16)
    r1 = x - hi.astype(F32)
    mid = r1.astype(BF16)
    lo = (r1 - mid.astype(F32)).astype(BF16)
    return hi, mid, lo


def _hgrn_kernel(q_ref, f_ref, i_ref, og_ref, lb_ref, ng_ref, s0_ref, o_ref, sfin_ref,
                 st_ref, g_ref, *, n_chunks, n_heads, dk):
    c = pl.program_id(1)

    @pl.when(c == 0)
    def _():
        st_ref[...] = s0_ref[...]

    lb = lb_ref[...]
    ng = ng_ref[...]
    row = lax.broadcasted_iota(jnp.int32, (HG_CHUNK, 1), 0)
    rr = lax.broadcasted_iota(jnp.int32, (HG_CHUNK, HG_CHUNK), 0)
    cc = lax.broadcasted_iota(jnp.int32, (HG_CHUNK, HG_CHUNK), 1)
    causal = cc <= rr
    tri = jnp.where(causal, 1.0, 0.0).astype(BF16)
    blk = row // HG_SUB

    def chunk(ch, carry):
        sl = pl.ds(pl.multiple_of(ch * HG_CHUNK, HG_CHUNK), HG_CHUNK)
        fp = f_ref[sl, :]
        q = q_ref[sl, :]
        v = i_ref[sl, :]
        sg = _sigmoid(fp)
        logf = jnp.log(lb + (1.0 - lb) * sg)
        k = (1.0 - lb) * (1.0 - sg)
        hi, mid, lo = _split3(logf)
        g_ref[...] = _dot(tri, hi) + _dot(tri, mid) + _dot(tri, lo)
        gc = g_ref[...]
        gend = g_ref[pl.ds(HG_CHUNK - 1, 1), :]
        bases = [jnp.zeros_like(gend)] + [g_ref[pl.ds(HG_SUB * i - 1, 1), :] for i in range(1, N_SUB)]
        bvec = bases[N_SUB - 1]
        for i in range(N_SUB - 2, -1, -1):
            bvec = jnp.where(blk == i, bases[i], bvec)
        qhat = q * jnp.exp(gc - bvec)
        kts = [jnp.where(row < HG_SUB * (i + 1), k * jnp.exp(bases[i] - gc), 0.0).astype(BF16)
               for i in range(N_SUB)]
        qi = (q * jnp.exp(gc)).astype(BF16)
        ks = (k * jnp.exp(gend - gc)).astype(BF16)
        dec = jnp.exp(gend)
        vb = v.astype(BF16)
        outs = []
        for h in range(n_heads):
            ls = slice(h * dk, (h + 1) * dk)
            qh = qhat[:, ls]
            qt = jnp.concatenate([jnp.where(blk == i, qh, 0.0) for i in range(N_SUB)], axis=1).astype(BF16)
            kt = jnp.concatenate([kts[i][:, ls] for i in range(N_SUB)], axis=1)
            sc = lax.dot_general(qt, kt, _NT, preferred_element_type=F32)
            p = jnp.where(causal, sc, 0.0).astype(BF16)
            st = st_ref[h]
            o_h = _dot(p, vb[:, ls]) + lax.dot_general(qi[:, ls], st.astype(BF16), _NT,
                                                       preferred_element_type=F32)
            st_ref[h] = st * dec[:, ls] + lax.dot_general(vb[:, ls], ks[:, ls], _TN,
                                                          preferred_element_type=F32)
            outs.append(o_h * lax.rsqrt(jnp.mean(o_h * o_h, axis=-1, keepdims=True) + EPS))
        o = jnp.concatenate(outs, axis=1)
        o_ref[sl, :] = o * ng * _sigmoid(og_ref[sl, :])
        return carry

    lax.fori_loop(0, n_chunks, chunk, 0)

    @pl.when(c == pl.num_programs(1) - 1)
    def _():
        sfin_ref[...] = st_ref[...]


def _hgrn(proj, lb, ng, s0, batch, t, tt, n_heads, dk):
    d_hg = n_heads * dk
    nc = t // tt
    col = lambda j: pl.BlockSpec((tt, d_hg), lambda b, c, j=j: (b * nc + c, j))
    kern = functools.partial(_hgrn_kernel, n_chunks=tt // HG_CHUNK, n_heads=n_heads, dk=dk)
    return pl.pallas_call(
        kern,
        grid=(batch, nc),
        in_specs=[col(0), col(1), col(2), col(3),
                  pl.BlockSpec((1, d_hg), lambda b, c: (0, 0)),
                  pl.BlockSpec((1, d_hg), lambda b, c: (0, 0)),
                  pl.BlockSpec((n_heads, dk, dk), lambda b, c: (0, 0, 0))],
        out_specs=[pl.BlockSpec((tt, d_hg), lambda b, c: (b * nc + c, 0)),
                   pl.BlockSpec((None, n_heads, dk, dk), lambda b, c: (b, 0, 0, 0))],
        out_shape=[jax.ShapeDtypeStruct((batch * t, d_hg), F32),
                   jax.ShapeDtypeStruct((batch, n_heads, dk, dk), F32)],
        scratch_shapes=[pltpu.VMEM((n_heads, dk, dk), F32),
                        pltpu.VMEM((HG_CHUNK, d_hg), F32)],
        compiler_params=_cparams(2),
        name="hgrn",
    )(proj, proj, proj, proj, lb, ng, s0)


def _softplus(x):
    return jnp.maximum(x, 0.0) + jnp.log(1.0 + jnp.exp(-jnp.abs(x)))


def _gelu_tanh(x):
    return 0.5 * x * (1.0 + jnp.tanh(0.7978845608028654 * (x + 0.044715 * (x * x * x))))


def _lru_kernel(lx_ref, ly_ref, cw_ref, cb_ref, wri_ref, br_ref, bi_ref, lam_ref, h0_ref, c0_ref,
                o_ref, hfin_ref, cfin_ref, ext_ref, h_ref, *, tt, n_blocks, bw, valid_from):
    c = pl.program_id(1)
    d = n_blocks * bw

    @pl.when(c == 0)
    def _():
        ext_ref[pl.ds(0, SUBLANES), :] = c0_ref[...]
        h_ref[...] = h0_ref[...]

    lx = lx_ref[...]
    ext_ref[pl.ds(SUBLANES, tt), :] = lx
    xc = cb_ref[...] + cw_ref[pl.ds(CONV_WIDTH - 1, 1), :] * lx
    for j in range(CONV_WIDTH - 1):
        xc = xc + cw_ref[pl.ds(j, 1), :] * ext_ref[pl.ds(SUBLANES - (CONV_WIDTH - 1) + j, tt), :]
    ext_ref[pl.ds(0, SUBLANES), :] = lx[tt - SUBLANES:, :]

    xcb = xc.astype(BF16)
    rs, is_ = [], []
    for n in range(n_blocks):
        ri = _dot(xcb[:, n * bw:(n + 1) * bw], wri_ref[n])
        rs.append(ri[:, :bw])
        is_.append(ri[:, bw:])
    r = _sigmoid(jnp.concatenate(rs, axis=1) + br_ref[...])
    ig = _sigmoid(jnp.concatenate(is_, axis=1) + bi_ref[...])
    log_a = (-LRU_C) * r * _softplus(-lam_ref[...])
    a = jnp.exp(log_a)
    th = jnp.tanh(log_a)
    u = jnp.sqrt(-2.0 * th / (1.0 - th)) * (ig * xc)
    row = lax.broadcasted_iota(jnp.int32, (tt, 1), 0)
    if valid_from:
        u = jnp.where(row >= valid_from, u, 0.0)
    s = 1
    while s < tt:
        keep = row >= s
        a_s = jnp.where(keep, pltpu.roll(a, s, 0), 1.0)
        u_s = jnp.where(keep, pltpu.roll(u, s, 0), 0.0)
        u = a * u_s + u
        a = a * a_s
        s *= 2
    h = u + a * h_ref[...]
    h_ref[...] = h[tt - 1:, :]
    o_ref[...] = h * _gelu_tanh(ly_ref[...])

    @pl.when(c == pl.num_programs(1) - 1)
    def _():
        hfin_ref[...] = h_ref[...]
        cfin_ref[...] = ext_ref[pl.ds(0, SUBLANES), :]


def _lru(proj, cw, cb, wri, br, bi, lam, h0, c0, batch, t, tt, col0, valid_from):
    n_blocks, bw, _ = wri.shape
    d = n_blocks * bw
    nc = t // tt
    vec = pl.BlockSpec((1, d), lambda b, c: (0, 0))
    kern = functools.partial(_lru_kernel, tt=tt, n_blocks=n_blocks, bw=bw, valid_from=valid_from)
    return pl.pallas_call(
        kern,
        grid=(batch, nc),
        in_specs=[pl.BlockSpec((tt, d), lambda b, c: (b * nc + c, col0)),
                  pl.BlockSpec((tt, d), lambda b, c: (b * nc + c, col0 + 1)),
                  pl.BlockSpec((CONV_WIDTH, d), lambda b, c: (0, 0)),
                  vec,
                  pl.BlockSpec((n_blocks, bw, 2 * bw), lambda b, c: (0, 0, 0)),
                  vec, vec, vec, vec,
                  pl.BlockSpec((SUBLANES, d), lambda b, c: (0, 0))],
        out_specs=[pl.BlockSpec((tt, d), lambda b, c: (b * nc + c, 0)),
                   pl.BlockSpec((None, 1, d), lambda b, c: (b, 0, 0)),
                   pl.BlockSpec((None, SUBLANES, d), lambda b, c: (b, 0, 0))],
        out_shape=[jax.ShapeDtypeStruct((batch * t, d), F32),
                   jax.ShapeDtypeStruct((batch, 1, d), F32),
                   jax.ShapeDtypeStruct((batch, SUBLANES, d), F32)],
        scratch_shapes=[pltpu.VMEM((tt + SUBLANES, d), F32),
                        pltpu.VMEM((1, d), F32)],
        compiler_params=_cparams(2),
        name="lru",
    )(proj, proj, cw, cb, wri, br, bi, lam, h0, c0)


def _merge_kernel(x_ref, oa_ref, ob_ref, ga_ref, gb_ref, wa_ref, wb_ref, wo_ref, g2_ref,
                  wrh_ref, wrl_ref, br_ref, h2_ref, hn_ref, route_ref, routet_ref, cnt_ref,
                  run_ref, *, n_groups, per_group):
    @pl.when(pl.program_id(0) == 0)
    def _():
        run_ref[...] = jnp.zeros_like(run_ref)

    up_a = _dot(oa_ref[...].astype(BF16), wa_ref[...])
    up_b = _dot(ob_ref[...].astype(BF16), wb_ref[...])
    merged = _sigmoid(ga_ref[...]) * up_a + _sigmoid(gb_ref[...]) * up_b
    h2 = x_ref[...] + _dot(merged.astype(BF16), wo_ref[...])
    h2_ref[...] = h2
    hn = _rms(h2, g2_ref[...])
    hn_ref[...] = hn
    hh = hn.astype(BF16)
    hl = (hn - hh.astype(F32)).astype(BF16)
    logits = _dot(hh, wrh_ref[...]) + _dot(hh, wrl_ref[...]) + _dot(hl, wrh_ref[...]) + br_ref[...]
    lane = lax.broadcasted_iota(jnp.int32, logits.shape, 1).astype(F32)
    neg = -jnp.inf
    big = float(LANES)

    def first_argmax(vals):
        m = jnp.max(vals, axis=1, keepdims=True)
        return m, jnp.min(jnp.where(vals == m, lane, big), axis=1, keepdims=True)

    is_g = lane < n_groups
    gmax, gidx = first_argmax(jnp.where(is_g, logits, neg))
    psel = 1.0 / jnp.sum(jnp.where(is_g, jnp.exp(logits - gmax), 0.0), axis=1, keepdims=True)
    lo = n_groups + per_group * gidx
    el = jnp.where((lane >= lo) & (lane < lo + per_group), logits, neg)
    v1, i1 = first_argmax(el)
    v2, i2 = first_argmax(jnp.where(lane == i1, neg, el))
    t = jnp.exp(v2 - v1)
    w1 = psel / (1.0 + t)
    w2 = psel * t / (1.0 + t)
    e1 = i1 - n_groups
    e2 = i2 - n_groups
    tm = logits.shape[0]
    onehot = jnp.where((lane == e1) | (lane == e2), 1.0, 0.0)
    rr = lax.broadcasted_iota(jnp.int32, (tm, tm), 0)
    cc = lax.broadcasted_iota(jnp.int32, (tm, tm), 1)
    before = jnp.where(cc < rr, 1.0, 0.0).astype(BF16)
    prefix = _dot(before, onehot.astype(BF16)) + run_ref[...]
    r1 = jnp.sum(jnp.where(lane == e1, prefix, 0.0), axis=1, keepdims=True)
    r2 = jnp.sum(jnp.where(lane == e2, prefix, 0.0), axis=1, keepdims=True)
    run_ref[...] = run_ref[...] + jnp.sum(onehot, axis=0, keepdims=True)
    cnt_ref[...] = run_ref[...]
    route = jnp.where(lane == 0, e1, jnp.where(lane == 1, e2, jnp.where(lane == 2, w1, jnp.where(
        lane == 3, w2, jnp.where(lane == 4, r1, jnp.where(lane == 5, r2, 0.0))))))
    route_ref[...] = route
    routet_ref[...] = jnp.transpose(route)[:SUBLANES, :]


def _merge(xf, o_a, o_b, proj, wa, wb, wo, g2, wrh, wrl, br, tm, gate_col0, n_groups, per_group):
    n, d = xf.shape
    d_hg = o_a.shape[1]
    d_lru = o_b.shape[1]
    full = lambda shape: pl.BlockSpec(shape, lambda i: (0,) * len(shape))
    rows = lambda w: pl.BlockSpec((tm, w), lambda i: (i, 0))
    kern = functools.partial(_merge_kernel, n_groups=n_groups, per_group=per_group)
    return pl.pallas_call(
        kern,
        grid=(n // tm,),
        in_specs=[rows(d), rows(d_hg), rows(d_lru),
                  pl.BlockSpec((tm, d), lambda i: (i, gate_col0)),
                  pl.BlockSpec((tm, d), lambda i: (i, gate_col0 + 1)),
                  full((d_hg, d)), full((d_lru, d)), full((d, d)), full((1, d)),
                  full((d, LANES)), full((d, LANES)), full((1, LANES))],
        out_specs=[rows(d), rows(d), rows(LANES),
                   pl.BlockSpec((None, SUBLANES, tm), lambda i: (i, 0, 0)),
                   pl.BlockSpec((1, LANES), lambda i: (0, 0))],
        out_shape=[jax.ShapeDtypeStruct((n, d), F32),
                   jax.ShapeDtypeStruct((n, d), F32),
                   jax.ShapeDtypeStruct((n, LANES), F32),
                   jax.ShapeDtypeStruct((n // tm, SUBLANES, tm), F32),
                   jax.ShapeDtypeStruct((1, LANES), F32)],
        scratch_shapes=[pltpu.VMEM((1, LANES), F32)],
        compiler_params=_cparams(1),
        name="merge",
    )(xf, o_a, o_b, proj, proj, wa, wb, wo, g2, wrh, wrl, br)


def _row_copy(src_hbm, row, dst, r, sem):
    return pltpu.make_async_copy(src_hbm.at[pl.ds(row, 1), :], dst.at[pl.ds(r, 1), :], sem)


DMA_UNROLL = 8


def _scatter_kernel(ts_ref, te_ref, nu_ref, p1_ref, p2_ref, hn_ref, xs_hbm, zbuf, sem, zsem, *, tm, bm,
                    n_experts, n_blocks):
    def zero_row(r):
        return pltpu.make_async_copy(zbuf.at[pl.ds(0, 1), :], xs_hbm.at[pl.ds(r, 1), :], zsem)

    def zero_block(m):
        return pltpu.make_async_copy(zbuf, xs_hbm.at[pl.ds(pl.multiple_of(m * bm, bm), bm), :], zsem)

    def start(cp):
        cp.start()
        return 0

    def wait(cp):
        cp.wait()
        return 0

    @pl.when(pl.program_id(0) == 0)
    def _():
        zbuf[...] = jnp.zeros_like(zbuf)
        for action in (start, wait):
            def tail(e, carry, action=action):
                return lax.fori_loop(ts_ref[e], te_ref[e], lambda r, c: action(zero_row(r)), carry)

            lax.fori_loop(0, n_experts, tail, 0)
            lax.fori_loop(nu_ref[0], n_blocks, lambda m, c: action(zero_block(m)), 0)

    def row_to(p_ref, r):
        return pltpu.make_async_copy(hn_ref.at[pl.ds(r, 1), :], xs_hbm.at[pl.ds(p_ref[0, r], 1), :], sem)

    def issue(r, carry):
        row_to(p1_ref, r).start()
        row_to(p2_ref, r).start()
        return carry

    lax.fori_loop(0, tm, issue, 0, unroll=DMA_UNROLL)

    def drain(r, carry):
        row_to(p1_ref, r).wait()
        row_to(p2_ref, r).wait()
        return carry

    lax.fori_loop(0, tm, drain, 0, unroll=DMA_UNROLL)


def _scatter(tail_start, tail_end, n_used, pos1, pos2, hn, n_blocks, bm, n_experts, tm):
    n, d = hn.shape
    smem_row = pl.BlockSpec((None, 1, tm), lambda i, *_: (i, 0, 0), memory_space=pltpu.SMEM)
    grid_spec = pltpu.PrefetchScalarGridSpec(
        num_scalar_prefetch=3,
        grid=(n // tm,),
        in_specs=[smem_row, smem_row, pl.BlockSpec((tm, d), lambda i, *_: (i, 0))],
        out_specs=pl.BlockSpec(memory_space=pl.ANY),
        scratch_shapes=[pltpu.VMEM((bm, d), F32), pltpu.SemaphoreType.DMA(()), pltpu.SemaphoreType.DMA(())],
    )
    return pl.pallas_call(
        functools.partial(_scatter_kernel, tm=tm, bm=bm, n_experts=n_experts, n_blocks=n_blocks),
        grid_spec=grid_spec,
        out_shape=jax.ShapeDtypeStruct((n_blocks * bm, d), F32),
        compiler_params=_cparams(1),
        name="scatter",
    )(tail_start, tail_end, n_used, pos1, pos2, hn)


def _moe_kernel(bexp_ref, bval_ref, bfirst_ref, bsrc_ref, xs_ref, wg_ref, wu_ref, wd_ref, y_ref,
                wg_b, wu_b, wd_b):
    m = pl.program_id(0)

    @pl.when(bfirst_ref[m] != 0)
    def _():
        wg_b[...] = wg_ref[...].astype(BF16)
        wu_b[...] = wu_ref[...].astype(BF16)
        wd_b[...] = wd_ref[...].astype(BF16)

    @pl.when(bval_ref[m] != 0)
    def _():
        xb = xs_ref[...].astype(BF16)
        g = _dot(xb, wg_b[...])
        u = _dot(xb, wu_b[...])
        hm = (g * _sigmoid(g) * u).astype(BF16)
        y_ref[...] = _dot(hm, wd_b[...])

    @pl.when(bval_ref[m] == 0)
    def _():
        y_ref[...] = jnp.zeros_like(y_ref)


def _moe(block_expert, block_valid, block_first, block_src, xs, wg, wu, wd, bm):
    n_blocks = block_expert.shape[0]
    d = xs.shape[1]
    de = wg.shape[2]
    grid_spec = pltpu.PrefetchScalarGridSpec(
        num_scalar_prefetch=4,
        grid=(n_blocks,),
        in_specs=[pl.BlockSpec((bm, d), lambda m, be, bv, bf, bs: (bs[m], 0)),
                  pl.BlockSpec((None, d, de), lambda m, be, bv, bf, bs: (be[m], 0, 0)),
                  pl.BlockSpec((None, d, de), lambda m, be, bv, bf, bs: (be[m], 0, 0)),
                  pl.BlockSpec((None, de, d), lambda m, be, bv, bf, bs: (be[m], 0, 0))],
        out_specs=pl.BlockSpec((bm, d), lambda m, be, bv, bf, bs: (m, 0)),
        scratch_shapes=[pltpu.VMEM((d, de), BF16), pltpu.VMEM((d, de), BF16), pltpu.VMEM((de, d), BF16)],
    )
    return pl.pallas_call(
        _moe_kernel,
        grid_spec=grid_spec,
        out_shape=jax.ShapeDtypeStruct((n_blocks * bm, d), F32),
        compiler_params=_cparams(1),
        name="moe",
    )(block_expert, block_valid, block_first, block_src, xs, wg, wu, wd)


def _combine_kernel(p1_ref, p2_ref, y_hbm, h2_ref, route_ref, fg_ref, o_ref, buf1, buf2, sem, *, tm):
    def issue(r, carry):
        _row_copy(y_hbm, p1_ref[0, r], buf1, r, sem).start()
        _row_copy(y_hbm, p2_ref[0, r], buf2, r, sem).start()
        return carry

    lax.fori_loop(0, tm, issue, 0, unroll=DMA_UNROLL)

    def wait(r, carry):
        _row_copy(y_hbm, 0, buf1, r, sem).wait()
        _row_copy(y_hbm, 0, buf2, r, sem).wait()
        return carry

    lax.fori_loop(0, tm, wait, 0, unroll=DMA_UNROLL)
    route = route_ref[...]
    ffn = buf1[...] * route[:, 2:3] + buf2[...] * route[:, 3:4]
    o_ref[...] = _rms(h2_ref[...] + ffn, fg_ref[...])


def _combine(pos1, pos2, y, h2, route, fg, tm):
    n, d = h2.shape
    smem_row = pl.BlockSpec((None, 1, tm), lambda i: (i, 0, 0), memory_space=pltpu.SMEM)
    return pl.pallas_call(
        functools.partial(_combine_kernel, tm=tm),
        grid=(n // tm,),
        in_specs=[smem_row, smem_row,
                  pl.BlockSpec(memory_space=pl.ANY),
                  pl.BlockSpec((tm, d), lambda i: (i, 0)),
                  pl.BlockSpec((tm, LANES), lambda i: (i, 0)),
                  pl.BlockSpec((1, d), lambda i: (0, 0))],
        out_specs=pl.BlockSpec((tm, d), lambda i: (i, 0)),
        out_shape=jax.ShapeDtypeStruct((n, d), F32),
        scratch_shapes=[pltpu.VMEM((tm, d), F32), pltpu.VMEM((tm, d), F32), pltpu.SemaphoreType.DMA(())],
        compiler_params=_cparams(1),
        name="combine",
    )(pos1, pos2, y, h2, route, fg)


def _layout(counts, n_blocks, bm):
    n_experts = counts.shape[0]
    pcounts = (counts + bm - 1) // bm * bm
    pends = jnp.cumsum(pcounts)
    pstarts = pends - pcounts
    n_used = pends[-1] // bm
    blocks = jnp.arange(n_blocks, dtype=jnp.int32)
    block_expert = jnp.minimum(jnp.sum((pends[None, :] <= (blocks * bm)[:, None]).astype(jnp.int32), axis=1),
                               n_experts - 1).astype(jnp.int32)
    block_valid = (blocks < n_used).astype(jnp.int32)
    block_first = ((blocks == 0) | (block_expert != jnp.roll(block_expert, 1))).astype(jnp.int32)
    block_src = jnp.minimum(blocks, n_used - 1).astype(jnp.int32)
    return pstarts, pstarts + counts, pends, n_used.reshape(1), block_expert, block_valid, block_first, block_src


def _slot_of(pstarts, e, rank):
    table = jnp.arange(pstarts.shape[0], dtype=jnp.int32)
    return jnp.sum(jnp.where(e[..., None] == table, pstarts, 0), axis=-1) + rank


def _tile(n, pref):
    t = pref
    while n % t:
        t //= 2
    return t


def kernel(x, meta_tokens, norm1_g, w_in, hg_lower_bounds, hg_norm_g, conv_w, conv_b, lru_w_r, lru_b_r,
           lru_w_i, lru_b_i, lru_lambda, w_up_a, w_up_b, w_out, norm2_g, w_group, b_group, w_router,
           b_router, w_gate, w_up, w_down, final_g):
    batch, t, d = x.shape
    n = batch * t
    n_meta = meta_tokens.shape[0]
    d_hg = hg_norm_g.shape[1]
    n_blocks_lru, bw = lru_w_r.shape[1], lru_w_r.shape[2]
    d_lru = n_blocks_lru * bw
    dk = LANES
    n_heads = d_hg // dk
    n_groups, per_group = w_router.shape[1], w_router.shape[3]
    n_experts = n_groups * per_group
    assert d_lru == d and d % LANES == 0 and t % HG_CHUNK == 0 and n_meta <= HG_CHUNK
    assert w_in.shape[2] == 4 * d_hg + 2 * d_lru + 2 * d
    assert n_groups + n_experts <= LANES

    layer = 0
    lbs = jnp.cumsum(jax.nn.softmax(hg_lower_bounds.astype(F32), axis=0), axis=0)
    lb = lbs[layer][None, :]
    g1 = norm1_g[layer][None, :]
    w_in_b = w_in[layer].astype(BF16)
    ng = hg_norm_g[layer][None, :]
    cw = conv_w[layer]
    cb = conv_b[layer][None, :]
    wri = jnp.concatenate([lru_w_r[layer], lru_w_i[layer]], axis=-1).astype(BF16)
    br = lru_b_r[layer][None, :]
    bi = lru_b_i[layer][None, :]
    lam = lru_lambda[layer][None, :]
    lx_col = (4 * d_hg) // d_lru
    gate_col = (4 * d_hg + 2 * d_lru) // d

    pad = HG_CHUNK - n_meta
    mseq = jnp.concatenate([jnp.zeros((pad, d), F32), meta_tokens.astype(F32)], axis=0)
    proj_m = _proj(mseq, g1, w_in_b, HG_CHUNK)
    _, s_meta = _hgrn(proj_m, lb, ng, jnp.zeros((n_heads, dk, dk), F32), 1, HG_CHUNK, HG_CHUNK, n_heads, dk)
    _, h_meta, c_meta = _lru(proj_m, cw, cb, wri, br, bi, lam, jnp.zeros((1, d_lru), F32),
                             jnp.zeros((SUBLANES, d_lru), F32), 1, HG_CHUNK, HG_CHUNK, lx_col, pad)

    xf = x.reshape(n, d)
    tm = _tile(n, 256)
    tt = _tile(t, 256)
    proj = _proj(xf, g1, w_in_b, tm)
    o_a, _ = _hgrn(proj, lb, ng, s_meta[0], batch, t, tt, n_heads, dk)
    o_b, _, _ = _lru(proj, cw, cb, wri, br, bi, lam, h_meta[0], c_meta[0], batch, t, tt, lx_col, 0)

    wr = jnp.concatenate([w_group[layer], jnp.transpose(w_router[layer], (1, 0, 2)).reshape(d, n_experts)], axis=1)
    wr = jnp.pad(wr, ((0, 0), (0, LANES - wr.shape[1])))
    wrh = wr.astype(BF16)
    wrl = (wr - wrh.astype(F32)).astype(BF16)
    brt = jnp.pad(jnp.concatenate([b_group[layer], b_router[layer].reshape(n_experts)]),
                  (0, LANES - n_groups - n_experts))[None, :]
    h2, hn, route, route_t, cnt = _merge(xf, o_a, o_b, proj, w_up_a[layer].astype(BF16),
                                         w_up_b[layer].astype(BF16), w_out[layer].astype(BF16),
                                         norm2_g[layer][None, :], wrh, wrl, brt, tm, gate_col, n_groups, per_group)

    bm = MOE_ROWS
    n_blocks = (2 * n) // bm + n_experts
    counts = cnt[0, :n_experts].astype(jnp.int32)
    pstarts, tail_start, tail_end, n_used, block_expert, block_valid, block_first, block_src = _layout(
        counts, n_blocks, bm)
    rt = route_t.astype(jnp.int32)
    pos1 = _slot_of(pstarts, rt[:, 0, :], rt[:, 4, :])[:, None, :]
    pos2 = _slot_of(pstarts, rt[:, 1, :], rt[:, 5, :])[:, None, :]
    xs = _scatter(tail_start, tail_end, n_used, pos1, pos2, hn, n_blocks, bm, n_experts, tm)
    y = _moe(block_expert, block_valid, block_first, block_src, xs, w_gate[layer], w_up[layer], w_down[layer], bm)
    out = _combine(pos1, pos2, y, h2, route, final_g[None, :], tm)
    return out.reshape(batch, t, d)
```

```python
import functools

import jax
import jax.numpy as jnp
from jax import lax
from jax.experimental import pallas as pl
from jax.experimental.pallas import tpu as pltpu

F32 = jnp.float32
BF16 = jnp.bfloat16
EPS = 1e-6
LRU_C = 8.0
CONV_WIDTH = 4
HG_CHUNK = 64
HG_SUB = 16
N_SUB = HG_CHUNK // HG_SUB
LANES = 128
SUBLANES = 8
MOE_ROWS = 512
VMEM_LIMIT_BYTES = 56 * 1024 * 1024

_NT = (((1,), (1,)), ((), ()))
_TN = (((0,), (0,)), ((), ()))


def _cparams(n_axes):
    return pltpu.CompilerParams(dimension_semantics=("arbitrary",) * n_axes,
                                vmem_limit_bytes=VMEM_LIMIT_BYTES)


def _dot(a, b):
    return jnp.dot(a, b, preferred_element_type=F32)


def _sigmoid(x):
    return 1.0 / (1.0 + jnp.exp(-x))


def _rms(x, g):
    return x * lax.rsqrt(jnp.mean(x * x, axis=-1, keepdims=True) + EPS) * g


def _proj_kernel(x_ref, g_ref, w_ref, o_ref):
    xn = _rms(x_ref[...], g_ref[...]).astype(BF16)
    o_ref[...] = _dot(xn, w_ref[...])


def _proj(xf, g, w_bf16, tm):
    n, d = xf.shape
    d_in = w_bf16.shape[1]
    return pl.pallas_call(
        _proj_kernel,
        grid=(n // tm,),
        in_specs=[pl.BlockSpec((tm, d), lambda i: (i, 0)),
                  pl.BlockSpec((1, d), lambda i: (0, 0)),
                  pl.BlockSpec((d, d_in), lambda i: (0, 0), pipeline_mode=pl.Buffered(1))],
        out_specs=pl.BlockSpec((tm, d_in), lambda i: (i, 0)),
        out_shape=jax.ShapeDtypeStruct((n, d_in), F32),
        compiler_params=_cparams(1),
        name="proj",
    )(xf, g, w_bf16)


def _split3(x):
    hi = x.astype(BF16)
    r1 = x - hi.astype(F32)
    mid = r1.astype(BF16)
    lo = (r1 - mid.astype(F32)).astype(BF16)
    return hi, mid, lo


def _hgrn_kernel(q_ref, f_ref, i_ref, og_ref, lb_ref, ng_ref, s0_ref, o_ref, sfin_ref,
                 st_ref, g_ref, *, n_chunks, n_heads, dk):
    c = pl.program_id(1)

    @pl.when(c == 0)
    def _():
        st_ref[...] = s0_ref[...]

    lb = lb_ref[...]
    ng = ng_ref[...]
    row = lax.broadcasted_iota(jnp.int32, (HG_CHUNK, 1), 0)
    rr = lax.broadcasted_iota(jnp.int32, (HG_CHUNK, HG_CHUNK), 0)
    cc = lax.broadcasted_iota(jnp.int32, (HG_CHUNK, HG_CHUNK), 1)
    causal = cc <= rr
    tri = jnp.where(causal, 1.0, 0.0).astype(BF16)
    blk = row // HG_SUB

    for ch in range(n_chunks):
        sl = pl.ds(ch * HG_CHUNK, HG_CHUNK)
        g0 = ch * HG_CHUNK
        fp = f_ref[sl, :]
        q = q_ref[sl, :]
        v = i_ref[sl, :]
        sg = _sigmoid(fp)
        logf = jnp.log(lb + (1.0 - lb) * sg)
        k = (1.0 - lb) * (1.0 - sg)
        hi, mid, lo = _split3(logf)
        g_ref[sl, :] = _dot(tri, hi) + _dot(tri, mid) + _dot(tri, lo)
        gc = g_ref[sl, :]
        gend = g_ref[pl.ds(g0 + HG_CHUNK - 1, 1), :]
        bases = [jnp.zeros_like(gend)] + [g_ref[pl.ds(g0 + HG_SUB * i - 1, 1), :] for i in range(1, N_SUB)]
        bvec = bases[N_SUB - 1]
        for i in range(N_SUB - 2, -1, -1):
            bvec = jnp.where(blk == i, bases[i], bvec)
        qhat = q * jnp.exp(gc - bvec)
        kts = [jnp.where(row < HG_SUB * (i + 1), k * jnp.exp(bases[i] - gc), 0.0).astype(BF16)
               for i in range(N_SUB)]
        qi = (q * jnp.exp(gc)).astype(BF16)
        ks = (k * jnp.exp(gend - gc)).astype(BF16)
        dec = jnp.exp(gend)
        vb = v.astype(BF16)
        outs = []
        for h in range(n_heads):
            ls = slice(h * dk, (h + 1) * dk)
            qh = qhat[:, ls]
            qt = jnp.concatenate([jnp.where(blk == i, qh, 0.0) for i in range(N_SUB)], axis=1).astype(BF16)
            kt = jnp.concatenate([kts[i][:, ls] for i in range(N_SUB)], axis=1)
            sc = lax.dot_general(qt, kt, _NT, preferred_element_type=F32)
            p = jnp.where(causal, sc, 0.0).astype(BF16)
            st = st_ref[h]
            o_h = _dot(p, vb[:, ls]) + lax.dot_general(qi[:, ls], st.astype(BF16), _NT,
                                                       preferred_element_type=F32)
            st_ref[h] = st * dec[:, ls] + lax.dot_general(vb[:, ls], ks[:, ls], _TN,
                                                          preferred_element_type=F32)
            outs.append(o_h * lax.rsqrt(jnp.mean(o_h * o_h, axis=-1, keepdims=True) + EPS))
        o = jnp.concatenate(outs, axis=1)
        o_ref[sl, :] = o * ng * _sigmoid(og_ref[sl, :])

    @pl.when(c == pl.num_programs(1) - 1)
    def _():
        sfin_ref[...] = st_ref[...]


def _hgrn(proj, lb, ng, s0, batch, t, tt, n_heads, dk):
    d_hg = n_heads * dk
    nc = t // tt
    col = lambda j: pl.BlockSpec((tt, d_hg), lambda b, c, j=j: (b * nc + c, j))
    kern = functools.partial(_hgrn_kernel, n_chunks=tt // HG_CHUNK, n_heads=n_heads, dk=dk)
    return pl.pallas_call(
        kern,
        grid=(batch, nc),
        in_specs=[col(0), col(1), col(2), col(3),
                  pl.BlockSpec((1, d_hg), lambda b, c: (0, 0)),
                  pl.BlockSpec((1, d_hg), lambda b, c: (0, 0)),
                  pl.BlockSpec((n_heads, dk, dk), lambda b, c: (0, 0, 0))],
        out_specs=[pl.BlockSpec((tt, d_hg), lambda b, c: (b * nc + c, 0)),
                   pl.BlockSpec((None, n_heads, dk, dk), lambda b, c: (b, 0, 0, 0))],
        out_shape=[jax.ShapeDtypeStruct((batch * t, d_hg), F32),
                   jax.ShapeDtypeStruct((batch, n_heads, dk, dk), F32)],
        scratch_shapes=[pltpu.VMEM((n_heads, dk, dk), F32),
                        pltpu.VMEM((tt, d_hg), F32)],
        compiler_params=_cparams(2),
        name="hgrn",
    )(proj, proj, proj, proj, lb, ng, s0)


def _softplus(x):
    return jnp.maximum(x, 0.0) + jnp.log(1.0 + jnp.exp(-jnp.abs(x)))


def _gelu_tanh(x):
    return 0.5 * x * (1.0 + jnp.tanh(0.7978845608028654 * (x + 0.044715 * (x * x * x))))


def _lru_kernel(lx_ref, ly_ref, cw_ref, cb_ref, wri_ref, br_ref, bi_ref, lam_ref, h0_ref, c0_ref,
                o_ref, hfin_ref, cfin_ref, ext_ref, h_ref, *, tt, n_blocks, bw, valid_from):
    c = pl.program_id(1)
    d = n_blocks * bw

    @pl.when(c == 0)
    def _():
        ext_ref[pl.ds(0, SUBLANES), :] = c0_ref[...]
        h_ref[...] = h0_ref[...]

    lx = lx_ref[...]
    ext = jnp.concatenate([ext_ref[...], lx], axis=0)
    xc = cb_ref[...] + cw_ref[pl.ds(CONV_WIDTH - 1, 1), :] * lx
    for j in range(1, CONV_WIDTH):
        xc = xc + cw_ref[pl.ds(CONV_WIDTH - 1 - j, 1), :] * pltpu.roll(ext, j, 0)[SUBLANES:, :]
    ext_ref[...] = lx[tt - SUBLANES:, :]

    xcb = xc.astype(BF16)
    rs, is_ = [], []
    for n in range(n_blocks):
        ri = _dot(xcb[:, n * bw:(n + 1) * bw], wri_ref[n])
        rs.append(ri[:, :bw])
        is_.append(ri[:, bw:])
    r = _sigmoid(jnp.concatenate(rs, axis=1) + br_ref[...])
    ig = _sigmoid(jnp.concatenate(is_, axis=1) + bi_ref[...])
    log_a = (-LRU_C) * r * _softplus(-lam_ref[...])
    a = jnp.exp(log_a)
    th = jnp.tanh(log_a)
    z = -2.0 * th / (1.0 - th)
    u = jnp.where(z > 0.0, z * lax.rsqrt(z), 0.0) * (ig * xc)
    row = lax.broadcasted_iota(jnp.int32, (tt, 1), 0)
    if valid_from:
        u = jnp.where(row >= valid_from, u, 0.0)
    groups = tt // SUBLANES
    a3 = a.reshape(groups, SUBLANES, d)
    u3 = u.reshape(groups, SUBLANES, d)
    sub = lax.broadcasted_iota(jnp.int32, (1, SUBLANES, 1), 1)
    s = 1
    while s < SUBLANES:
        keep = sub >= s
        a_s = jnp.where(keep, pltpu.roll(a3, s, 1), 1.0)
        u_s = jnp.where(keep, pltpu.roll(u3, s, 1), 0.0)
        u3 = a3 * u_s + u3
        a3 = a3 * a_s
        s *= 2
    h_prev = h_ref[...]
    ly = ly_ref[...]
    for g in range(groups):
        hg = u3[g] + a3[g] * h_prev
        h_prev = hg[SUBLANES - 1:, :]
        rows = pl.ds(g * SUBLANES, SUBLANES)
        o_ref[rows, :] = hg * _gelu_tanh(ly[g * SUBLANES:(g + 1) * SUBLANES, :])
    h_ref[...] = h_prev

    @pl.when(c == pl.num_programs(1) - 1)
    def _():
        hfin_ref[...] = h_ref[...]
        cfin_ref[...] = ext_ref[pl.ds(0, SUBLANES), :]


def _lru(proj, cw, cb, wri, br, bi, lam, h0, c0, batch, t, tt, col0, valid_from):
    n_blocks, bw, _ = wri.shape
    d = n_blocks * bw
    nc = t // tt
    vec = pl.BlockSpec((1, d), lambda b, c: (0, 0))
    kern = functools.partial(_lru_kernel, tt=tt, n_blocks=n_blocks, bw=bw, valid_from=valid_from)
    return pl.pallas_call(
        kern,
        grid=(batch, nc),
        in_specs=[pl.BlockSpec((tt, d), lambda b, c: (b * nc + c, col0)),
                  pl.BlockSpec((tt, d), lambda b, c: (b * nc + c, col0 + 1)),
                  pl.BlockSpec((CONV_WIDTH, d), lambda b, c: (0, 0)),
                  vec,
                  pl.BlockSpec((n_blocks, bw, 2 * bw), lambda b, c: (0, 0, 0)),
                  vec, vec, vec, vec,
                  pl.BlockSpec((SUBLANES, d), lambda b, c: (0, 0))],
        out_specs=[pl.BlockSpec((tt, d), lambda b, c: (b * nc + c, 0)),
                   pl.BlockSpec((None, 1, d), lambda b, c: (b, 0, 0)),
                   pl.BlockSpec((None, SUBLANES, d), lambda b, c: (b, 0, 0))],
        out_shape=[jax.ShapeDtypeStruct((batch * t, d), F32),
                   jax.ShapeDtypeStruct((batch, 1, d), F32),
                   jax.ShapeDtypeStruct((batch, SUBLANES, d), F32)],
        scratch_shapes=[pltpu.VMEM((SUBLANES, d), F32),
                        pltpu.VMEM((1, d), F32)],
        compiler_params=_cparams(2),
        name="lru",
    )(proj, proj, cw, cb, wri, br, bi, lam, h0, c0)


def _merge_kernel(x_ref, oa_ref, ob_ref, ga_ref, gb_ref, wa_ref, wb_ref, wo_ref, g2_ref,
                  wrh_ref, wrl_ref, br_ref, h2_ref, hn_ref, route_ref, routet_ref, cnt_ref,
                  run_ref, *, n_groups, per_group):
    @pl.when(pl.program_id(0) == 0)
    def _():
        run_ref[...] = jnp.zeros_like(run_ref)

    up_a = _dot(oa_ref[...].astype(BF16), wa_ref[...])
    up_b = _dot(ob_ref[...].astype(BF16), wb_ref[...])
    merged = _sigmoid(ga_ref[...]) * up_a + _sigmoid(gb_ref[...]) * up_b
    h2 = x_ref[...] + _dot(merged.astype(BF16), wo_ref[...])
    h2_ref[...] = h2
    hn = _rms(h2, g2_ref[...])
    hn_ref[...] = hn
    hh = hn.astype(BF16)
    hl = (hn - hh.astype(F32)).astype(BF16)
    logits = _dot(hh, wrh_ref[...]) + _dot(hh, wrl_ref[...]) + _dot(hl, wrh_ref[...]) + br_ref[...]
    lane = lax.broadcasted_iota(jnp.int32, logits.shape, 1).astype(F32)
    neg = -jnp.inf
    big = float(LANES)

    def first_argmax(vals):
        m = jnp.max(vals, axis=1, keepdims=True)
        return m, jnp.min(jnp.where(vals == m, lane, big), axis=1, keepdims=True)

    is_g = lane < n_groups
    gmax, gidx = first_argmax(jnp.where(is_g, logits, neg))
    psel = 1.0 / jnp.sum(jnp.where(is_g, jnp.exp(logits - gmax), 0.0), axis=1, keepdims=True)
    lo = n_groups + per_group * gidx
    el = jnp.where((lane >= lo) & (lane < lo + per_group), logits, neg)
    v1, i1 = first_argmax(el)
    v2, i2 = first_argmax(jnp.where(lane == i1, neg, el))
    t = jnp.exp(v2 - v1)
    w1 = psel / (1.0 + t)
    w2 = psel * t / (1.0 + t)
    e1 = i1 - n_groups
    e2 = i2 - n_groups
    tm = logits.shape[0]
    onehot = jnp.where((lane == e1) | (lane == e2), 1.0, 0.0)
    rr = lax.broadcasted_iota(jnp.int32, (tm, tm), 0)
    cc = lax.broadcasted_iota(jnp.int32, (tm, tm), 1)
    before = jnp.where(cc < rr, 1.0, 0.0).astype(BF16)
    prefix = _dot(before, onehot.astype(BF16)) + run_ref[...]
    r1 = jnp.sum(jnp.where(lane == e1, prefix, 0.0), axis=1, keepdims=True)
    r2 = jnp.sum(jnp.where(lane == e2, prefix, 0.0), axis=1, keepdims=True)
    run_ref[...] = run_ref[...] + jnp.sum(onehot, axis=0, keepdims=True)
    cnt_ref[...] = run_ref[...]
    route = jnp.where(lane == 0, e1, jnp.where(lane == 1, e2, jnp.where(lane == 2, w1, jnp.where(
        lane == 3, w2, jnp.where(lane == 4, r1, jnp.where(lane == 5, r2, 0.0))))))
    route_ref[...] = route
    routet_ref[...] = jnp.transpose(route)[:SUBLANES, :]


def _merge(xf, o_a, o_b, proj, wa, wb, wo, g2, wrh, wrl, br, tm, gate_col0, n_groups, per_group):
    n, d = xf.shape
    d_hg = o_a.shape[1]
    d_lru = o_b.shape[1]
    full = lambda shape: pl.BlockSpec(shape, lambda i: (0,) * len(shape))
    rows = lambda w: pl.BlockSpec((tm, w), lambda i: (i, 0))
    kern = functools.partial(_merge_kernel, n_groups=n_groups, per_group=per_group)
    return pl.pallas_call(
        kern,
        grid=(n // tm,),
        in_specs=[rows(d), rows(d_hg), rows(d_lru),
                  pl.BlockSpec((tm, d), lambda i: (i, gate_col0)),
                  pl.BlockSpec((tm, d), lambda i: (i, gate_col0 + 1)),
                  full((d_hg, d)), full((d_lru, d)), full((d, d)), full((1, d)),
                  full((d, LANES)), full((d, LANES)), full((1, LANES))],
        out_specs=[rows(d), rows(d), rows(LANES),
                   pl.BlockSpec((None, SUBLANES, tm), lambda i: (i, 0, 0)),
                   pl.BlockSpec((1, LANES), lambda i: (0, 0))],
        out_shape=[jax.ShapeDtypeStruct((n, d), F32),
                   jax.ShapeDtypeStruct((n, d), F32),
                   jax.ShapeDtypeStruct((n, LANES), F32),
                   jax.ShapeDtypeStruct((n // tm, SUBLANES, tm), F32),
                   jax.ShapeDtypeStruct((1, LANES), F32)],
        scratch_shapes=[pltpu.VMEM((1, LANES), F32)],
        compiler_params=_cparams(1),
        name="merge",
    )(xf, o_a, o_b, proj, proj, wa, wb, wo, g2, wrh, wrl, br)


def _row_copy(src_hbm, row, dst, r, sem):
    return pltpu.make_async_copy(src_hbm.at[pl.ds(row, 1), :], dst.at[pl.ds(r, 1), :], sem)


DMA_UNROLL = 16


def _scatter_kernel(ts_ref, te_ref, nu_ref, p1_ref, p2_ref, hn_ref, xs_hbm, zbuf, sem, zsem, *, tm, bm,
                    n_experts, n_blocks):
    def zero_row(r):
        return pltpu.make_async_copy(zbuf.at[pl.ds(0, 1), :], xs_hbm.at[pl.ds(r, 1), :], zsem)

    def zero_block(m):
        return pltpu.make_async_copy(zbuf, xs_hbm.at[pl.ds(pl.multiple_of(m * bm, bm), bm), :], zsem)

    def start(cp):
        cp.start()
        return 0

    def wait(cp):
        cp.wait()
        return 0

    @pl.when(pl.program_id(0) == 0)
    def _():
        zbuf[...] = jnp.zeros_like(zbuf)
        for action in (start, wait):
            def tail(e, carry, action=action):
                return lax.fori_loop(ts_ref[e], te_ref[e], lambda r, c: action(zero_row(r)), carry)

            lax.fori_loop(0, n_experts, tail, 0)
            lax.fori_loop(nu_ref[0], n_blocks, lambda m, c: action(zero_block(m)), 0)

    def row_to(p_ref, r):
        return pltpu.make_async_copy(hn_ref.at[pl.ds(r, 1), :], xs_hbm.at[pl.ds(p_ref[0, r], 1), :], sem)

    def issue(r, carry):
        row_to(p1_ref, r).start(priority=0)
        row_to(p2_ref, r).start(priority=1)
        return carry

    lax.fori_loop(0, tm, issue, 0, unroll=DMA_UNROLL)

    def drain(r, carry):
        row_to(p1_ref, r).wait()
        row_to(p2_ref, r).wait()
        return carry

    lax.fori_loop(0, tm, drain, 0, unroll=DMA_UNROLL)


def _scatter(tail_start, tail_end, n_used, pos1, pos2, hn, n_blocks, bm, n_experts, tm):
    n, d = hn.shape
    smem_row = pl.BlockSpec((None, 1, tm), lambda i, *_: (i, 0, 0), memory_space=pltpu.SMEM)
    grid_spec = pltpu.PrefetchScalarGridSpec(
        num_scalar_prefetch=3,
        grid=(n // tm,),
        in_specs=[smem_row, smem_row, pl.BlockSpec((tm, d), lambda i, *_: (i, 0))],
        out_specs=pl.BlockSpec(memory_space=pl.ANY),
        scratch_shapes=[pltpu.VMEM((bm, d), F32), pltpu.SemaphoreType.DMA(()), pltpu.SemaphoreType.DMA(())],
    )
    return pl.pallas_call(
        functools.partial(_scatter_kernel, tm=tm, bm=bm, n_experts=n_experts, n_blocks=n_blocks),
        grid_spec=grid_spec,
        out_shape=jax.ShapeDtypeStruct((n_blocks * bm, d), F32),
        compiler_params=_cparams(1),
        name="scatter",
    )(tail_start, tail_end, n_used, pos1, pos2, hn)


def _moe_kernel(bexp_ref, bval_ref, bfirst_ref, bsrc_ref, xs_ref, wg_ref, wu_ref, wd_ref, y_ref,
                wg_b, wu_b, wd_b):
    m = pl.program_id(0)

    @pl.when(bfirst_ref[m] != 0)
    def _():
        wg_b[...] = wg_ref[...].astype(BF16)
        wu_b[...] = wu_ref[...].astype(BF16)
        wd_b[...] = wd_ref[...].astype(BF16)

    @pl.when(bval_ref[m] != 0)
    def _():
        xb = xs_ref[...].astype(BF16)
        g = _dot(xb, wg_b[...])
        u = _dot(xb, wu_b[...])
        hm = (g * _sigmoid(g) * u).astype(BF16)
        y_ref[...] = _dot(hm, wd_b[...])

    @pl.when(bval_ref[m] == 0)
    def _():
        y_ref[...] = jnp.zeros_like(y_ref)


def _moe(block_expert, block_valid, block_first, block_src, xs, wg, wu, wd, bm):
    n_blocks = block_expert.shape[0]
    d = xs.shape[1]
    de = wg.shape[2]
    grid_spec = pltpu.PrefetchScalarGridSpec(
        num_scalar_prefetch=4,
        grid=(n_blocks,),
        in_specs=[pl.BlockSpec((bm, d), lambda m, be, bv, bf, bs: (bs[m], 0)),
                  pl.BlockSpec((None, d, de), lambda m, be, bv, bf, bs: (be[m], 0, 0)),
                  pl.BlockSpec((None, d, de), lambda m, be, bv, bf, bs: (be[m], 0, 0)),
                  pl.BlockSpec((None, de, d), lambda m, be, bv, bf, bs: (be[m], 0, 0))],
        out_specs=pl.BlockSpec((bm, d), lambda m, be, bv, bf, bs: (m, 0)),
        scratch_shapes=[pltpu.VMEM((d, de), BF16), pltpu.VMEM((d, de), BF16), pltpu.VMEM((de, d), BF16)],
    )
    return pl.pallas_call(
        _moe_kernel,
        grid_spec=grid_spec,
        out_shape=jax.ShapeDtypeStruct((n_blocks * bm, d), F32),
        compiler_params=_cparams(1),
        name="moe",
    )(block_expert, block_valid, block_first, block_src, xs, wg, wu, wd)


def _combine_kernel(p1_ref, p2_ref, y_hbm, h2_ref, route_ref, fg_ref, o_ref, buf1, buf2, sem, *, tm):
    def issue(r, carry):
        _row_copy(y_hbm, p1_ref[0, r], buf1, r, sem).start(priority=0)
        _row_copy(y_hbm, p2_ref[0, r], buf2, r, sem).start(priority=1)
        return carry

    lax.fori_loop(0, tm, issue, 0, unroll=DMA_UNROLL)

    def wait(r, carry):
        _row_copy(y_hbm, 0, buf1, r, sem).wait()
        _row_copy(y_hbm, 0, buf2, r, sem).wait()
        return carry

    lax.fori_loop(0, tm, wait, 0, unroll=DMA_UNROLL)
    route = route_ref[...]
    ffn = buf1[...] * route[:, 2:3] + buf2[...] * route[:, 3:4]
    o_ref[...] = _rms(h2_ref[...] + ffn, fg_ref[...])


def _combine(pos1, pos2, y, h2, route, fg, tm):
    n, d = h2.shape
    smem_row = pl.BlockSpec((None, 1, tm), lambda i: (i, 0, 0), memory_space=pltpu.SMEM)
    return pl.pallas_call(
        functools.partial(_combine_kernel, tm=tm),
        grid=(n // tm,),
        in_specs=[smem_row, smem_row,
                  pl.BlockSpec(memory_space=pl.ANY),
                  pl.BlockSpec((tm, d), lambda i: (i, 0)),
                  pl.BlockSpec((tm, LANES), lambda i: (i, 0)),
                  pl.BlockSpec((1, d), lambda i: (0, 0))],
        out_specs=pl.BlockSpec((tm, d), lambda i: (i, 0)),
        out_shape=jax.ShapeDtypeStruct((n, d), F32),
        scratch_shapes=[pltpu.VMEM((tm, d), F32), pltpu.VMEM((tm, d), F32), pltpu.SemaphoreType.DMA(())],
        compiler_params=_cparams(1),
        name="combine",
    )(pos1, pos2, y, h2, route, fg)


def _layout(counts, n_blocks, bm):
    n_experts = counts.shape[0]
    pcounts = (counts + bm - 1) // bm * bm
    pends = jnp.cumsum(pcounts)
    pstarts = pends - pcounts
    n_used = pends[-1] // bm
    blocks = jnp.arange(n_blocks, dtype=jnp.int32)
    block_expert = jnp.minimum(jnp.sum((pends[None, :] <= (blocks * bm)[:, None]).astype(jnp.int32), axis=1),
                               n_experts - 1).astype(jnp.int32)
    block_valid = (blocks < n_used).astype(jnp.int32)
    block_first = ((blocks == 0) | (block_expert != jnp.roll(block_expert, 1))).astype(jnp.int32)
    block_src = jnp.minimum(blocks, n_used - 1).astype(jnp.int32)
    return pstarts, pstarts + counts, pends, n_used.reshape(1), block_expert, block_valid, block_first, block_src


def _slot_of(pstarts, e, rank):
    table = jnp.arange(pstarts.shape[0], dtype=jnp.int32)
    return jnp.sum(jnp.where(e[..., None] == table, pstarts, 0), axis=-1) + rank


def _tile(n, pref):
    t = pref
    while n % t:
        t //= 2
    return t


def kernel(x, meta_tokens, norm1_g, w_in, hg_lower_bounds, hg_norm_g, conv_w, conv_b, lru_w_r, lru_b_r,
           lru_w_i, lru_b_i, lru_lambda, w_up_a, w_up_b, w_out, norm2_g, w_group, b_group, w_router,
           b_router, w_gate, w_up, w_down, final_g):
    batch, t, d = x.shape
    n = batch * t
    n_meta = meta_tokens.shape[0]
    d_hg = hg_norm_g.shape[1]
    n_blocks_lru, bw = lru_w_r.shape[1], lru_w_r.shape[2]
    d_lru = n_blocks_lru * bw
    dk = LANES
    n_heads = d_hg // dk
    n_groups, per_group = w_router.shape[1], w_router.shape[3]
    n_experts = n_groups * per_group
    assert d_lru == d and d % LANES == 0 and t % HG_CHUNK == 0 and n_meta <= HG_CHUNK
    assert w_in.shape[2] == 4 * d_hg + 2 * d_lru + 2 * d
    assert n_groups + n_experts <= LANES

    layer = 0
    lbs = jnp.cumsum(jax.nn.softmax(hg_lower_bounds.astype(F32), axis=0), axis=0)
    lb = lbs[layer][None, :]
    g1 = norm1_g[layer][None, :]
    w_in_b = w_in[layer].astype(BF16)
    ng = hg_norm_g[layer][None, :]
    cw = conv_w[layer]
    cb = conv_b[layer][None, :]
    wri = jnp.concatenate([lru_w_r[layer], lru_w_i[layer]], axis=-1).astype(BF16)
    br = lru_b_r[layer][None, :]
    bi = lru_b_i[layer][None, :]
    lam = lru_lambda[layer][None, :]
    lx_col = (4 * d_hg) // d_lru
    gate_col = (4 * d_hg + 2 * d_lru) // d

    pad = HG_CHUNK - n_meta
    mseq = jnp.concatenate([jnp.zeros((pad, d), F32), meta_tokens.astype(F32)], axis=0)
    proj_m = _proj(mseq, g1, w_in_b, HG_CHUNK)
    _, s_meta = _hgrn(proj_m, lb, ng, jnp.zeros((n_heads, dk, dk), F32), 1, HG_CHUNK, HG_CHUNK, n_heads, dk)
    _, h_meta, c_meta = _lru(proj_m, cw, cb, wri, br, bi, lam, jnp.zeros((1, d_lru), F32),
                             jnp.zeros((SUBLANES, d_lru), F32), 1, HG_CHUNK, HG_CHUNK, lx_col, pad)

    xf = x.reshape(n, d)
    tm = _tile(n, 256)
    tt = _tile(t, 256)
    proj = _proj(xf, g1, w_in_b, tm)
    o_a, _ = _hgrn(proj, lb, ng, s_meta[0], batch, t, tt, n_heads, dk)
    o_b, _, _ = _lru(proj, cw, cb, wri, br, bi, lam, h_meta[0], c_meta[0], batch, t, tt, lx_col, 0)

    wr = jnp.concatenate([w_group[layer], jnp.transpose(w_router[layer], (1, 0, 2)).reshape(d, n_experts)], axis=1)
    wr = jnp.pad(wr, ((0, 0), (0, LANES - wr.shape[1])))
    wrh = wr.astype(BF16)
    wrl = (wr - wrh.astype(F32)).astype(BF16)
    brt = jnp.pad(jnp.concatenate([b_group[layer], b_router[layer].reshape(n_experts)]),
                  (0, LANES - n_groups - n_experts))[None, :]
    h2, hn, route, route_t, cnt = _merge(xf, o_a, o_b, proj, w_up_a[layer].astype(BF16),
                                         w_up_b[layer].astype(BF16), w_out[layer].astype(BF16),
                                         norm2_g[layer][None, :], wrh, wrl, brt, tm, gate_col, n_groups, per_group)

    bm = MOE_ROWS
    n_blocks = (2 * n) // bm + n_experts
    counts = cnt[0, :n_experts].astype(jnp.int32)
    pstarts, tail_start, tail_end, n_used, block_expert, block_valid, block_first, block_src = _layout(
        counts, n_blocks, bm)
    rt = route_t.astype(jnp.int32)
    pos1 = _slot_of(pstarts, rt[:, 0, :], rt[:, 4, :])[:, None, :]
    pos2 = _slot_of(pstarts, rt[:, 1, :], rt[:, 5, :])[:, None, :]
    xs = _scatter(tail_start, tail_end, n_used, pos1, pos2, hn, n_blocks, bm, n_experts, tm)
    y = _moe(block_expert, block_valid, block_first, block_src, xs, w_gate[layer], w_up[layer], w_down[layer], bm)
    out = _combine(pos1, pos2, y, h2, route, final_g[None, :], tm)
    return out.reshape(batch, t, d)
```

```python
import functools

import jax
import jax.numpy as jnp
from jax import lax
from jax.experimental import pallas as pl
from jax.experimental.pallas import tpu as pltpu

F32 = jnp.float32
BF16 = jnp.bfloat16
EPS = 1e-6
LRU_C = 8.0
CONV_WIDTH = 4
HG_CHUNK = 64
HG_SUB = 16
N_SUB = HG_CHUNK // HG_SUB
LANES = 128
SUBLANES = 8
MOE_ROWS = 512
VMEM_LIMIT_BYTES = 56 * 1024 * 1024

_NT = (((1,), (1,)), ((), ()))
_TN = (((0,), (0,)), ((), ()))


def _cparams(n_axes):
    return pltpu.CompilerParams(dimension_semantics=("arbitrary",) * n_axes,
                                vmem_limit_bytes=VMEM_LIMIT_BYTES)


def _dot(a, b):
    return jnp.dot(a, b, preferred_element_type=F32)


def _sigmoid(x):
    return 1.0 / (1.0 + jnp.exp(-x))


def _rms(x, g):
    return x * lax.rsqrt(jnp.mean(x * x, axis=-1, keepdims=True) + EPS) * g


def _store_tile_rows(ref, x):
    rows = x.shape[0]
    for j in range(SUBLANES):
        ref[pl.ds(j, rows, stride=SUBLANES), :] = x[:, j * LANES:(j + 1) * LANES]


def _load_tile_rows(ref):
    rows = ref.shape[0] // SUBLANES
    return jnp.concatenate([ref[pl.ds(j, rows, stride=SUBLANES), :] for j in range(SUBLANES)], axis=1)


def _tile_row(ref, r):
    start = r * SUBLANES if isinstance(r, int) else pl.multiple_of(r * SUBLANES, SUBLANES)
    return ref.at[pl.ds(start, SUBLANES), :]


def _proj_kernel(x_ref, g_ref, w_ref, o_ref):
    xn = _rms(x_ref[...], g_ref[...]).astype(BF16)
    o_ref[...] = _dot(xn, w_ref[...])


def _proj(xf, g, w_bf16, tm):
    n, d = xf.shape
    d_in = w_bf16.shape[1]
    return pl.pallas_call(
        _proj_kernel,
        grid=(n // tm,),
        in_specs=[pl.BlockSpec((tm, d), lambda i: (i, 0)),
                  pl.BlockSpec((1, d), lambda i: (0, 0)),
                  pl.BlockSpec((d, d_in), lambda i: (0, 0), pipeline_mode=pl.Buffered(1))],
        out_specs=pl.BlockSpec((tm, d_in), lambda i: (i, 0)),
        out_shape=jax.ShapeDtypeStruct((n, d_in), F32),
        compiler_params=_cparams(1),
        name="proj",
    )(xf, g, w_bf16)


def _split3(x):
    hi = x.astype(BF16)
    r1 = x - hi.astype(F32)
    mid = r1.astype(BF16)
    lo = (r1 - mid.astype(F32)).astype(BF16)
    return hi, mid, lo


def _hgrn_kernel(q_ref, f_ref, i_ref, og_ref, lb_ref, ng_ref, s0_ref, o_ref, sfin_ref,
                 st_ref, g_ref, *, n_chunks, n_heads, dk):
    c = pl.program_id(1)

    @pl.when(c == 0)
    def _():
        st_ref[...] = s0_ref[...]

    lb = lb_ref[...]
    ng = ng_ref[...]
    row = lax.broadcasted_iota(jnp.int32, (HG_CHUNK, 1), 0)
    rr = lax.broadcasted_iota(jnp.int32, (HG_CHUNK, HG_CHUNK), 0)
    cc = lax.broadcasted_iota(jnp.int32, (HG_CHUNK, HG_CHUNK), 1)
    causal = cc <= rr
    tri = jnp.where(causal, 1.0, 0.0).astype(BF16)
    blk = row // HG_SUB

    for ch in range(n_chunks):
        sl = pl.ds(ch * HG_CHUNK, HG_CHUNK)
        g0 = ch * HG_CHUNK
        fp = f_ref[sl, :]
        q = q_ref[sl, :]
        v = i_ref[sl, :]
        sg = _sigmoid(fp)
        logf = jnp.log(lb + (1.0 - lb) * sg)
        k = (1.0 - lb) * (1.0 - sg)
        hi, mid, lo = _split3(logf)
        g_ref[sl, :] = _dot(tri, hi) + _dot(tri, mid) + _dot(tri, lo)
        gc = g_ref[sl, :]
        gend = g_ref[pl.ds(g0 + HG_CHUNK - 1, 1), :]
        bases = [jnp.zeros_like(gend)] + [g_ref[pl.ds(g0 + HG_SUB * i - 1, 1), :] for i in range(1, N_SUB)]
        bvec = bases[N_SUB - 1]
        for i in range(N_SUB - 2, -1, -1):
            bvec = jnp.where(blk == i, bases[i], bvec)
        qhat = q * jnp.exp(gc - bvec)
        kts = [jnp.where(row < HG_SUB * (i + 1), k * jnp.exp(bases[i] - gc), 0.0).astype(BF16)
               for i in range(N_SUB)]
        qi = (q * jnp.exp(gc)).astype(BF16)
        ks = (k * jnp.exp(gend - gc)).astype(BF16)
        dec = jnp.exp(gend)
        vb = v.astype(BF16)
        outs = []
        for h in range(n_heads):
            ls = slice(h * dk, (h + 1) * dk)
            qh = qhat[:, ls]
            qt = jnp.concatenate([jnp.where(blk == i, qh, 0.0) for i in range(N_SUB)], axis=1).astype(BF16)
            kt = jnp.concatenate([kts[i][:, ls] for i in range(N_SUB)], axis=1)
            sc = lax.dot_general(qt, kt, _NT, preferred_element_type=F32)
            p = jnp.where(causal, sc, 0.0).astype(BF16)
            st = st_ref[h]
            o_h = _dot(p, vb[:, ls]) + lax.dot_general(qi[:, ls], st.astype(BF16), _NT,
                                                       preferred_element_type=F32)
            st_ref[h] = st * dec[:, ls] + lax.dot_general(vb[:, ls], ks[:, ls], _TN,
                                                          preferred_element_type=F32)
            outs.append(o_h * lax.rsqrt(jnp.mean(o_h * o_h, axis=-1, keepdims=True) + EPS))
        o = jnp.concatenate(outs, axis=1)
        o_ref[sl, :] = o * ng * _sigmoid(og_ref[sl, :])

    @pl.when(c == pl.num_programs(1) - 1)
    def _():
        sfin_ref[...] = st_ref[...]


def _hgrn(proj, lb, ng, s0, batch, t, tt, n_heads, dk):
    d_hg = n_heads * dk
    nc = t // tt
    col = lambda j: pl.BlockSpec((tt, d_hg), lambda b, c, j=j: (b * nc + c, j))
    kern = functools.partial(_hgrn_kernel, n_chunks=tt // HG_CHUNK, n_heads=n_heads, dk=dk)
    return pl.pallas_call(
        kern,
        grid=(batch, nc),
        in_specs=[col(0), col(1), col(2), col(3),
                  pl.BlockSpec((1, d_hg), lambda b, c: (0, 0)),
                  pl.BlockSpec((1, d_hg), lambda b, c: (0, 0)),
                  pl.BlockSpec((n_heads, dk, dk), lambda b, c: (0, 0, 0))],
        out_specs=[pl.BlockSpec((tt, d_hg), lambda b, c: (b * nc + c, 0)),
                   pl.BlockSpec((None, n_heads, dk, dk), lambda b, c: (b, 0, 0, 0))],
        out_shape=[jax.ShapeDtypeStruct((batch * t, d_hg), F32),
                   jax.ShapeDtypeStruct((batch, n_heads, dk, dk), F32)],
        scratch_shapes=[pltpu.VMEM((n_heads, dk, dk), F32),
                        pltpu.VMEM((tt, d_hg), F32)],
        compiler_params=_cparams(2),
        name="hgrn",
    )(proj, proj, proj, proj, lb, ng, s0)


def _softplus(x):
    return jnp.maximum(x, 0.0) + jnp.log(1.0 + jnp.exp(-jnp.abs(x)))


def _gelu_tanh(x):
    return 0.5 * x * (1.0 + jnp.tanh(0.7978845608028654 * (x + 0.044715 * (x * x * x))))


def _lru_kernel(lx_ref, ly_ref, cw_ref, cb_ref, wri_ref, br_ref, bi_ref, lam_ref, h0_ref, c0_ref,
                o_ref, hfin_ref, cfin_ref, ext_ref, h_ref, *, tt, n_blocks, bw, valid_from):
    c = pl.program_id(1)
    d = n_blocks * bw

    @pl.when(c == 0)
    def _():
        ext_ref[pl.ds(0, SUBLANES), :] = c0_ref[...]
        h_ref[...] = h0_ref[...]

    lx = lx_ref[...]
    ext = jnp.concatenate([ext_ref[...], lx], axis=0)
    xc = cb_ref[...] + cw_ref[pl.ds(CONV_WIDTH - 1, 1), :] * lx
    for j in range(1, CONV_WIDTH):
        xc = xc + cw_ref[pl.ds(CONV_WIDTH - 1 - j, 1), :] * pltpu.roll(ext, j, 0)[SUBLANES:, :]
    ext_ref[...] = lx[tt - SUBLANES:, :]

    xcb = xc.astype(BF16)
    rs, is_ = [], []
    for n in range(n_blocks):
        ri = _dot(xcb[:, n * bw:(n + 1) * bw], wri_ref[n])
        rs.append(ri[:, :bw])
        is_.append(ri[:, bw:])
    r = _sigmoid(jnp.concatenate(rs, axis=1) + br_ref[...])
    ig = _sigmoid(jnp.concatenate(is_, axis=1) + bi_ref[...])
    log_a = (-LRU_C) * r * _softplus(-lam_ref[...])
    a = jnp.exp(log_a)
    th = jnp.tanh(log_a)
    z = -2.0 * th / (1.0 - th)
    u = jnp.where(z > 0.0, z * lax.rsqrt(z), 0.0) * (ig * xc)
    row = lax.broadcasted_iota(jnp.int32, (tt, 1), 0)
    if valid_from:
        u = jnp.where(row >= valid_from, u, 0.0)
    groups = tt // SUBLANES
    a3 = a.reshape(groups, SUBLANES, d)
    u3 = u.reshape(groups, SUBLANES, d)
    sub = lax.broadcasted_iota(jnp.int32, (1, SUBLANES, 1), 1)
    s = 1
    while s < SUBLANES:
        keep = sub >= s
        a_s = jnp.where(keep, pltpu.roll(a3, s, 1), 1.0)
        u_s = jnp.where(keep, pltpu.roll(u3, s, 1), 0.0)
        u3 = a3 * u_s + u3
        a3 = a3 * a_s
        s *= 2
    h_prev = h_ref[...]
    ly = ly_ref[...]
    for g in range(groups):
        hg = u3[g] + a3[g] * h_prev
        h_prev = hg[SUBLANES - 1:, :]
        rows = pl.ds(g * SUBLANES, SUBLANES)
        o_ref[rows, :] = hg * _gelu_tanh(ly[g * SUBLANES:(g + 1) * SUBLANES, :])
    h_ref[...] = h_prev

    @pl.when(c == pl.num_programs(1) - 1)
    def _():
        hfin_ref[...] = h_ref[...]
        cfin_ref[...] = ext_ref[pl.ds(0, SUBLANES), :]


def _lru(proj, cw, cb, wri, br, bi, lam, h0, c0, batch, t, tt, col0, valid_from):
    n_blocks, bw, _ = wri.shape
    d = n_blocks * bw
    nc = t // tt
    vec = pl.BlockSpec((1, d), lambda b, c: (0, 0))
    kern = functools.partial(_lru_kernel, tt=tt, n_blocks=n_blocks, bw=bw, valid_from=valid_from)
    return pl.pallas_call(
        kern,
        grid=(batch, nc),
        in_specs=[pl.BlockSpec((tt, d), lambda b, c: (b * nc + c, col0)),
                  pl.BlockSpec((tt, d), lambda b, c: (b * nc + c, col0 + 1)),
                  pl.BlockSpec((CONV_WIDTH, d), lambda b, c: (0, 0)),
                  vec,
                  pl.BlockSpec((n_blocks, bw, 2 * bw), lambda b, c: (0, 0, 0)),
                  vec, vec, vec, vec,
                  pl.BlockSpec((SUBLANES, d), lambda b, c: (0, 0))],
        out_specs=[pl.BlockSpec((tt, d), lambda b, c: (b * nc + c, 0)),
                   pl.BlockSpec((None, 1, d), lambda b, c: (b, 0, 0)),
                   pl.BlockSpec((None, SUBLANES, d), lambda b, c: (b, 0, 0))],
        out_shape=[jax.ShapeDtypeStruct((batch * t, d), F32),
                   jax.ShapeDtypeStruct((batch, 1, d), F32),
                   jax.ShapeDtypeStruct((batch, SUBLANES, d), F32)],
        scratch_shapes=[pltpu.VMEM((SUBLANES, d), F32),
                        pltpu.VMEM((1, d), F32)],
        compiler_params=_cparams(2),
        name="lru",
    )(proj, proj, cw, cb, wri, br, bi, lam, h0, c0)


def _merge_kernel(x_ref, oa_ref, ob_ref, ga_ref, gb_ref, wa_ref, wb_ref, wo_ref, g2_ref,
                  wrh_ref, wrl_ref, br_ref, h2_ref, hn_ref, route_ref, routet_ref, cnt_ref,
                  run_ref, *, n_groups, per_group):
    @pl.when(pl.program_id(0) == 0)
    def _():
        run_ref[...] = jnp.zeros_like(run_ref)

    up_a = _dot(oa_ref[...].astype(BF16), wa_ref[...])
    up_b = _dot(ob_ref[...].astype(BF16), wb_ref[...])
    merged = _sigmoid(ga_ref[...]) * up_a + _sigmoid(gb_ref[...]) * up_b
    h2 = x_ref[...] + _dot(merged.astype(BF16), wo_ref[...])
    h2_ref[...] = h2
    hn = _rms(h2, g2_ref[...])
    _store_tile_rows(hn_ref, hn)
    hh = hn.astype(BF16)
    hl = (hn - hh.astype(F32)).astype(BF16)
    logits = _dot(hh, wrh_ref[...]) + _dot(hh, wrl_ref[...]) + _dot(hl, wrh_ref[...]) + br_ref[...]
    lane = lax.broadcasted_iota(jnp.int32, logits.shape, 1).astype(F32)
    neg = -jnp.inf
    big = float(LANES)

    def first_argmax(vals):
        m = jnp.max(vals, axis=1, keepdims=True)
        return m, jnp.min(jnp.where(vals == m, lane, big), axis=1, keepdims=True)

    is_g = lane < n_groups
    gmax, gidx = first_argmax(jnp.where(is_g, logits, neg))
    psel = 1.0 / jnp.sum(jnp.where(is_g, jnp.exp(logits - gmax), 0.0), axis=1, keepdims=True)
    lo = n_groups + per_group * gidx
    el = jnp.where((lane >= lo) & (lane < lo + per_group), logits, neg)
    v1, i1 = first_argmax(el)
    v2, i2 = first_argmax(jnp.where(lane == i1, neg, el))
    t = jnp.exp(v2 - v1)
    w1 = psel / (1.0 + t)
    w2 = psel * t / (1.0 + t)
    e1 = i1 - n_groups
    e2 = i2 - n_groups
    tm = logits.shape[0]
    onehot = jnp.where((lane == e1) | (lane == e2), 1.0, 0.0)
    rr = lax.broadcasted_iota(jnp.int32, (tm, tm), 0)
    cc = lax.broadcasted_iota(jnp.int32, (tm, tm), 1)
    before = jnp.where(cc < rr, 1.0, 0.0).astype(BF16)
    prefix = _dot(before, onehot.astype(BF16)) + run_ref[...]
    r1 = jnp.sum(jnp.where(lane == e1, prefix, 0.0), axis=1, keepdims=True)
    r2 = jnp.sum(jnp.where(lane == e2, prefix, 0.0), axis=1, keepdims=True)
    run_ref[...] = run_ref[...] + jnp.sum(onehot, axis=0, keepdims=True)
    cnt_ref[...] = run_ref[...]
    route = jnp.where(lane == 0, e1, jnp.where(lane == 1, e2, jnp.where(lane == 2, w1, jnp.where(
        lane == 3, w2, jnp.where(lane == 4, r1, jnp.where(lane == 5, r2, 0.0))))))
    route_ref[...] = route
    routet_ref[...] = jnp.transpose(route)[:SUBLANES, :]


def _merge(xf, o_a, o_b, proj, wa, wb, wo, g2, wrh, wrl, br, tm, gate_col0, n_groups, per_group):
    n, d = xf.shape
    d_hg = o_a.shape[1]
    d_lru = o_b.shape[1]
    full = lambda shape: pl.BlockSpec(shape, lambda i: (0,) * len(shape))
    rows = lambda w: pl.BlockSpec((tm, w), lambda i: (i, 0))
    kern = functools.partial(_merge_kernel, n_groups=n_groups, per_group=per_group)
    return pl.pallas_call(
        kern,
        grid=(n // tm,),
        in_specs=[rows(d), rows(d_hg), rows(d_lru),
                  pl.BlockSpec((tm, d), lambda i: (i, gate_col0)),
                  pl.BlockSpec((tm, d), lambda i: (i, gate_col0 + 1)),
                  full((d_hg, d)), full((d_lru, d)), full((d, d)), full((1, d)),
                  full((d, LANES)), full((d, LANES)), full((1, LANES))],
        out_specs=[rows(d), pl.BlockSpec((tm * SUBLANES, LANES), lambda i: (i, 0)), rows(LANES),
                   pl.BlockSpec((None, SUBLANES, tm), lambda i: (i, 0, 0)),
                   pl.BlockSpec((1, LANES), lambda i: (0, 0))],
        out_shape=[jax.ShapeDtypeStruct((n, d), F32),
                   jax.ShapeDtypeStruct((n * SUBLANES, LANES), F32),
                   jax.ShapeDtypeStruct((n, LANES), F32),
                   jax.ShapeDtypeStruct((n // tm, SUBLANES, tm), F32),
                   jax.ShapeDtypeStruct((1, LANES), F32)],
        scratch_shapes=[pltpu.VMEM((1, LANES), F32)],
        compiler_params=_cparams(1),
        name="merge",
    )(xf, o_a, o_b, proj, proj, wa, wb, wo, g2, wrh, wrl, br)


def _row_copy(src, row, dst, r, sem):
    return pltpu.make_async_copy(_tile_row(src, row), _tile_row(dst, r), sem)


DMA_UNROLL = 16


def _scatter_kernel(ts_ref, te_ref, nu_ref, p1_ref, p2_ref, hn_ref, xs_hbm, zbuf, sem, zsem, *, tm, bm,
                    n_experts, n_blocks):
    def zero_row(r):
        return _row_copy(zbuf, 0, xs_hbm, r, zsem)

    def zero_block(m):
        rows = bm * SUBLANES
        return pltpu.make_async_copy(zbuf, xs_hbm.at[pl.ds(pl.multiple_of(m * rows, rows), rows), :], zsem)

    def start(cp):
        cp.start()
        return 0

    def wait(cp):
        cp.wait()
        return 0

    @pl.when(pl.program_id(0) == 0)
    def _():
        zbuf[...] = jnp.zeros_like(zbuf)
        for action in (start, wait):
            def tail(e, carry, action=action):
                return lax.fori_loop(ts_ref[e], te_ref[e], lambda r, c: action(zero_row(r)), carry)

            lax.fori_loop(0, n_experts, tail, 0)
            lax.fori_loop(nu_ref[0], n_blocks, lambda m, c: action(zero_block(m)), 0)

    def row_to(p_ref, r):
        return _row_copy(hn_ref, r, xs_hbm, p_ref[0, r], sem)

    def issue(r, carry):
        row_to(p1_ref, r).start(priority=0)
        row_to(p2_ref, r).start(priority=1)
        return carry

    lax.fori_loop(0, tm, issue, 0, unroll=DMA_UNROLL)

    def drain(r, carry):
        row_to(p1_ref, r).wait()
        row_to(p2_ref, r).wait()
        return carry

    lax.fori_loop(0, tm, drain, 0, unroll=DMA_UNROLL)


def _scatter(tail_start, tail_end, n_used, pos1, pos2, hn, n_blocks, bm, n_experts, tm):
    n = hn.shape[0] // SUBLANES
    smem_row = pl.BlockSpec((None, 1, tm), lambda i, *_: (i, 0, 0), memory_space=pltpu.SMEM)
    grid_spec = pltpu.PrefetchScalarGridSpec(
        num_scalar_prefetch=3,
        grid=(n // tm,),
        in_specs=[smem_row, smem_row, pl.BlockSpec((tm * SUBLANES, LANES), lambda i, *_: (i, 0))],
        out_specs=pl.BlockSpec(memory_space=pl.ANY),
        scratch_shapes=[pltpu.VMEM((bm * SUBLANES, LANES), F32), pltpu.SemaphoreType.DMA(()),
                        pltpu.SemaphoreType.DMA(())],
    )
    return pl.pallas_call(
        functools.partial(_scatter_kernel, tm=tm, bm=bm, n_experts=n_experts, n_blocks=n_blocks),
        grid_spec=grid_spec,
        out_shape=jax.ShapeDtypeStruct((n_blocks * bm * SUBLANES, LANES), F32),
        compiler_params=_cparams(1),
        name="scatter",
    )(tail_start, tail_end, n_used, pos1, pos2, hn)


def _moe_kernel(bexp_ref, bval_ref, bfirst_ref, bsrc_ref, xs_ref, wg_ref, wu_ref, wd_ref, y_ref,
                wg_b, wu_b, wd_b):
    m = pl.program_id(0)

    @pl.when(bfirst_ref[m] != 0)
    def _():
        wg_b[...] = wg_ref[...].astype(BF16)
        wu_b[...] = wu_ref[...].astype(BF16)
        wd_b[...] = wd_ref[...].astype(BF16)

    @pl.when(bval_ref[m] != 0)
    def _():
        xb = _load_tile_rows(xs_ref).astype(BF16)
        g = _dot(xb, wg_b[...])
        u = _dot(xb, wu_b[...])
        hm = (g * _sigmoid(g) * u).astype(BF16)
        _store_tile_rows(y_ref, _dot(hm, wd_b[...]))

    @pl.when(bval_ref[m] == 0)
    def _():
        y_ref[...] = jnp.zeros_like(y_ref)


def _moe(block_expert, block_valid, block_first, block_src, xs, wg, wu, wd, bm):
    n_blocks = block_expert.shape[0]
    d, de = wg.shape[1], wg.shape[2]
    assert d == SUBLANES * LANES
    tile_rows = pl.BlockSpec((bm * SUBLANES, LANES), lambda m, be, bv, bf, bs: (bs[m], 0))
    grid_spec = pltpu.PrefetchScalarGridSpec(
        num_scalar_prefetch=4,
        grid=(n_blocks,),
        in_specs=[tile_rows,
                  pl.BlockSpec((None, d, de), lambda m, be, bv, bf, bs: (be[m], 0, 0)),
                  pl.BlockSpec((None, d, de), lambda m, be, bv, bf, bs: (be[m], 0, 0)),
                  pl.BlockSpec((None, de, d), lambda m, be, bv, bf, bs: (be[m], 0, 0))],
        out_specs=pl.BlockSpec((bm * SUBLANES, LANES), lambda m, be, bv, bf, bs: (m, 0)),
        scratch_shapes=[pltpu.VMEM((d, de), BF16), pltpu.VMEM((d, de), BF16), pltpu.VMEM((de, d), BF16)],
    )
    return pl.pallas_call(
        _moe_kernel,
        grid_spec=grid_spec,
        out_shape=jax.ShapeDtypeStruct((n_blocks * bm * SUBLANES, LANES), F32),
        compiler_params=_cparams(1),
        name="moe",
    )(block_expert, block_valid, block_first, block_src, xs, wg, wu, wd)


def _combine_kernel(p1_ref, p2_ref, y_hbm, h2_ref, route_ref, fg_ref, o_ref, buf1, buf2, sem, *, tm):
    def issue(r, carry):
        _row_copy(y_hbm, p1_ref[0, r], buf1, r, sem).start(priority=0)
        _row_copy(y_hbm, p2_ref[0, r], buf2, r, sem).start(priority=1)
        return carry

    lax.fori_loop(0, tm, issue, 0, unroll=DMA_UNROLL)

    def wait(r, carry):
        _row_copy(y_hbm, 0, buf1, r, sem).wait()
        _row_copy(y_hbm, 0, buf2, r, sem).wait()
        return carry

    lax.fori_loop(0, tm, wait, 0, unroll=DMA_UNROLL)
    route = route_ref[...]
    ffn = _load_tile_rows(buf1) * route[:, 2:3] + _load_tile_rows(buf2) * route[:, 3:4]
    o_ref[...] = _rms(h2_ref[...] + ffn, fg_ref[...])


def _combine(pos1, pos2, y, h2, route, fg, tm):
    n, d = h2.shape
    smem_row = pl.BlockSpec((None, 1, tm), lambda i: (i, 0, 0), memory_space=pltpu.SMEM)
    return pl.pallas_call(
        functools.partial(_combine_kernel, tm=tm),
        grid=(n // tm,),
        in_specs=[smem_row, smem_row,
                  pl.BlockSpec(memory_space=pl.ANY),
                  pl.BlockSpec((tm, d), lambda i: (i, 0)),
                  pl.BlockSpec((tm, LANES), lambda i: (i, 0)),
                  pl.BlockSpec((1, d), lambda i: (0, 0))],
        out_specs=pl.BlockSpec((tm, d), lambda i: (i, 0)),
        out_shape=jax.ShapeDtypeStruct((n, d), F32),
        scratch_shapes=[pltpu.VMEM((tm * SUBLANES, LANES), F32), pltpu.VMEM((tm * SUBLANES, LANES), F32),
                        pltpu.SemaphoreType.DMA(())],
        compiler_params=_cparams(1),
        name="combine",
    )(pos1, pos2, y, h2, route, fg)


def _layout(counts, n_blocks, bm):
    n_experts = counts.shape[0]
    pcounts = (counts + bm - 1) // bm * bm
    pends = jnp.cumsum(pcounts)
    pstarts = pends - pcounts
    n_used = pends[-1] // bm
    blocks = jnp.arange(n_blocks, dtype=jnp.int32)
    block_expert = jnp.minimum(jnp.sum((pends[None, :] <= (blocks * bm)[:, None]).astype(jnp.int32), axis=1),
                               n_experts - 1).astype(jnp.int32)
    block_valid = (blocks < n_used).astype(jnp.int32)
    block_first = ((blocks == 0) | (block_expert != jnp.roll(block_expert, 1))).astype(jnp.int32)
    block_src = jnp.minimum(blocks, n_used - 1).astype(jnp.int32)
    return pstarts, pstarts + counts, pends, n_used.reshape(1), block_expert, block_valid, block_first, block_src


def _slot_of(pstarts, e, rank):
    table = jnp.arange(pstarts.shape[0], dtype=jnp.int32)
    return jnp.sum(jnp.where(e[..., None] == table, pstarts, 0), axis=-1) + rank


def _tile(n, pref):
    t = pref
    while n % t:
        t //= 2
    return t


def kernel(x, meta_tokens, norm1_g, w_in, hg_lower_bounds, hg_norm_g, conv_w, conv_b, lru_w_r, lru_b_r,
           lru_w_i, lru_b_i, lru_lambda, w_up_a, w_up_b, w_out, norm2_g, w_group, b_group, w_router,
           b_router, w_gate, w_up, w_down, final_g):
    batch, t, d = x.shape
    n = batch * t
    n_meta = meta_tokens.shape[0]
    d_hg = hg_norm_g.shape[1]
    n_blocks_lru, bw = lru_w_r.shape[1], lru_w_r.shape[2]
    d_lru = n_blocks_lru * bw
    dk = LANES
    n_heads = d_hg // dk
    n_groups, per_group = w_router.shape[1], w_router.shape[3]
    n_experts = n_groups * per_group
    assert d_lru == d and d % LANES == 0 and t % HG_CHUNK == 0 and n_meta <= HG_CHUNK
    assert w_in.shape[2] == 4 * d_hg + 2 * d_lru + 2 * d
    assert n_groups + n_experts <= LANES

    layer = 0
    lbs = jnp.cumsum(jax.nn.softmax(hg_lower_bounds.astype(F32), axis=0), axis=0)
    lb = lbs[layer][None, :]
    g1 = norm1_g[layer][None, :]
    w_in_b = w_in[layer].astype(BF16)
    ng = hg_norm_g[layer][None, :]
    cw = conv_w[layer]
    cb = conv_b[layer][None, :]
    wri = jnp.concatenate([lru_w_r[layer], lru_w_i[layer]], axis=-1).astype(BF16)
    br = lru_b_r[layer][None, :]
    bi = lru_b_i[layer][None, :]
    lam = lru_lambda[layer][None, :]
    lx_col = (4 * d_hg) // d_lru
    gate_col = (4 * d_hg + 2 * d_lru) // d

    pad = HG_CHUNK - n_meta
    mseq = jnp.concatenate([jnp.zeros((pad, d), F32), meta_tokens.astype(F32)], axis=0)
    proj_m = _proj(mseq, g1, w_in_b, HG_CHUNK)
    _, s_meta = _hgrn(proj_m, lb, ng, jnp.zeros((n_heads, dk, dk), F32), 1, HG_CHUNK, HG_CHUNK, n_heads, dk)
    _, h_meta, c_meta = _lru(proj_m, cw, cb, wri, br, bi, lam, jnp.zeros((1, d_lru), F32),
                             jnp.zeros((SUBLANES, d_lru), F32), 1, HG_CHUNK, HG_CHUNK, lx_col, pad)

    xf = x.reshape(n, d)
    tm = _tile(n, 256)
    tt = _tile(t, 256)
    proj = _proj(xf, g1, w_in_b, tm)
    o_a, _ = _hgrn(proj, lb, ng, s_meta[0], batch, t, tt, n_heads, dk)
    o_b, _, _ = _lru(proj, cw, cb, wri, br, bi, lam, h_meta[0], c_meta[0], batch, t, tt, lx_col, 0)

    wr = jnp.concatenate([w_group[layer], jnp.transpose(w_router[layer], (1, 0, 2)).reshape(d, n_experts)], axis=1)
    wr = jnp.pad(wr, ((0, 0), (0, LANES - wr.shape[1])))
    wrh = wr.astype(BF16)
    wrl = (wr - wrh.astype(F32)).astype(BF16)
    brt = jnp.pad(jnp.concatenate([b_group[layer], b_router[layer].reshape(n_experts)]),
                  (0, LANES - n_groups - n_experts))[None, :]
    h2, hn, route, route_t, cnt = _merge(xf, o_a, o_b, proj, w_up_a[layer].astype(BF16),
                                         w_up_b[layer].astype(BF16), w_out[layer].astype(BF16),
                                         norm2_g[layer][None, :], wrh, wrl, brt, tm, gate_col, n_groups, per_group)

    bm = MOE_ROWS
    n_blocks = (2 * n) // bm + n_experts
    counts = cnt[0, :n_experts].astype(jnp.int32)
    pstarts, tail_start, tail_end, n_used, block_expert, block_valid, block_first, block_src = _layout(
        counts, n_blocks, bm)
    rt = route_t.astype(jnp.int32)
    pos1 = _slot_of(pstarts, rt[:, 0, :], rt[:, 4, :])[:, None, :]
    pos2 = _slot_of(pstarts, rt[:, 1, :], rt[:, 5, :])[:, None, :]
    xs = _scatter(tail_start, tail_end, n_used, pos1, pos2, hn, n_blocks, bm, n_experts, tm)
    y = _moe(block_expert, block_valid, block_first, block_src, xs, w_gate[layer], w_up[layer], w_down[layer], bm)
    out = _combine(pos1, pos2, y, h2, route, final_g[None, :], tm)
    return out.reshape(batch, t, d)
```
